```python
import math
import jax, jax.numpy as jnp
from jax import lax
import numpy as np

D_MODEL = 1024
BATCH = 4
SEQ = 8192
DEPTH = 4

PLE_DIM = 256
BRANCH_W = 512
N_BRANCH = 3
A_HEADS = 8
A_NOPE = 64
A_ROPE = 32
A_VDIM = 64
A_Q_LORA = 256
A_KV_LORA = 128
A_SCALE = (A_NOPE + A_ROPE) ** -0.5
ROPE_THETA = 10000.0
Q_BLOCK = 128
B_HEADS = 4
B_DK = 128
B_DV = 128
B_CONV = 4
B_CHUNK = 64
C_HEADS = 4
C_DK = 128
C_DV = 128
C_CHUNK = 32

NORM_EPS = 1e-6
MASK_VALUE = -1e30
DEEPNORM_ALPHA = (2.0 * DEPTH) ** 0.25
DEEPNORM_BETA = (8.0 * DEPTH) ** -0.25

IN_SPLITS = (
    A_Q_LORA,
    A_KV_LORA,
    A_ROPE,
    BRANCH_W,
    3 * B_HEADS * B_DK,
    B_HEADS,
    B_HEADS,
    BRANCH_W,
    C_HEADS * C_DK,
    C_HEADS * C_DK,
    C_HEADS * C_DV,
    BRANCH_W,
    N_BRANCH * D_MODEL,
)
IN_WIDTH = int(sum(IN_SPLITS))
IN_OFFSETS = tuple(int(o) for o in np.cumsum(IN_SPLITS)[:-1])

kernel_name = 'hybrid_mla_gdn_hgrn2_block'


def rms_norm(x, g):
    xf = x.astype(jnp.float32)
    y = xf * lax.rsqrt(jnp.mean(xf * xf, axis=-1, keepdims=True) + NORM_EPS)
    return y.astype(x.dtype) * g


def layer_norm(x, g, b):
    xf = x.astype(jnp.float32)
    mu = jnp.mean(xf, axis=-1, keepdims=True)
    var = jnp.mean(jnp.square(xf - mu), axis=-1, keepdims=True)
    return ((xf - mu) * lax.rsqrt(var + NORM_EPS)).astype(x.dtype) * g + b


def l2_norm(x):
    return x * lax.rsqrt(jnp.sum(x * x, axis=-1, keepdims=True) + NORM_EPS)


def masked_exp(diff, mask):
    return jnp.where(mask, jnp.exp(jnp.where(mask, diff, 0.0)), 0.0)


def rope_tables(positions):
    inv = ROPE_THETA ** (-jnp.arange(0, A_ROPE, 2, dtype=jnp.float32) / A_ROPE)
    ang = positions.astype(jnp.float32)[..., None] * inv
    return jnp.cos(ang), jnp.sin(ang)


def apply_rope(x, cos, sin):
    xf = x.astype(jnp.float32)
    half = xf.shape[-1] // 2
    x1, x2 = xf[..., :half], xf[..., half:]
    return jnp.concatenate([x1 * cos - x2 * sin, x2 * cos + x1 * sin], axis=-1).astype(x.dtype)


def causal_depthwise_conv(x, w):
    width = w.shape[0]
    return lax.conv_general_dilated(
        x, w.astype(x.dtype), window_strides=(1,), padding=[(width - 1, 0)],
        dimension_numbers=('NWC', 'WIO', 'NWC'), feature_group_count=x.shape[-1])


def to_chunks(t, c):
    b, s, h = t.shape[:3]
    return jnp.moveaxis(t.reshape(b, s // c, c, h, *t.shape[3:]), 3, 1)


def from_chunks(o):
    n, b, h, c, d = o.shape
    return jnp.transpose(o, (1, 0, 3, 2, 4)).reshape(b, n * c, h, d)


def causal_block_attention(q_nope, q_rope, k_nope, k_rope, v):
    b, s, h, _ = q_nope.shape
    nblk = s // Q_BLOCK
    key_idx = jnp.arange(s)

    def blocks(t):
        return jnp.moveaxis(t.reshape(b, nblk, Q_BLOCK, *t.shape[2:]), 1, 0)

    def one_block(args):
        qn, qr, start = args
        scores = (jnp.einsum('bqhd,bkhd->bhqk', qn, k_nope)
                  + jnp.einsum('bqhd,bkd->bhqk', qr, k_rope)).astype(jnp.float32) * A_SCALE
        q_idx = start + jnp.arange(Q_BLOCK)
        mask = key_idx[None, :] <= q_idx[:, None]
        probs = jax.nn.softmax(jnp.where(mask, scores, MASK_VALUE), axis=-1).astype(v.dtype)
        return jnp.einsum('bhqk,bkhd->bqhd', probs, v)

    starts = jnp.arange(nblk, dtype=jnp.int32) * Q_BLOCK
    out = lax.map(one_block, (blocks(q_nope), blocks(q_rope), starts))
    return jnp.moveaxis(out, 0, 1).reshape(b, s, h, v.shape[-1])


def chunk_gated_delta_rule(q, k, v, g, beta):
    b, s, h, dk = q.shape
    dv = v.shape[-1]
    c = B_CHUNK
    q = to_chunks(q, c) * dk ** -0.5
    k = to_chunks(k, c)
    v = to_chunks(v, c)
    beta = to_chunks(beta, c)
    g = jnp.cumsum(to_chunks(g, c), axis=-1)
    tril = jnp.tril(jnp.ones((c, c), bool))
    strict = jnp.tril(jnp.ones((c, c), bool), -1)
    decay = masked_exp(g[..., :, None] - g[..., None, :], tril)
    k_beta = k * beta[..., None]
    low = jnp.where(strict, jnp.einsum('bhnid,bhnjd->bhnij', k_beta, k) * decay, 0.0)
    a_mat = low + jnp.eye(c, dtype=low.dtype)
    rhs = jnp.concatenate([v * beta[..., None], k_beta * jnp.exp(g)[..., None]], axis=-1)
    sol = lax.linalg.triangular_solve(a_mat, rhs, left_side=True, lower=True, unit_diagonal=True)
    u, w = sol[..., :dv], sol[..., dv:]
    intra = jnp.einsum('bhnid,bhnjd->bhnij', q, k) * decay
    q_dec = q * jnp.exp(g)[..., None]
    g_last = g[..., -1]
    k_dec = k * jnp.exp(g_last[..., None] - g)[..., None]

    def step(state, xs):
        u_c, w_c, a_c, qd_c, kd_c, gl_c = xs
        v_new = u_c - jnp.einsum('bhid,bhde->bhie', w_c, state)
        out = (jnp.einsum('bhid,bhde->bhie', qd_c, state)
               + jnp.einsum('bhij,bhje->bhie', a_c, v_new))
        state = state * jnp.exp(gl_c)[..., None, None] + jnp.einsum('bhid,bhie->bhde', kd_c, v_new)
        return state, out

    xs = tuple(jnp.moveaxis(t, 2, 0) for t in (u, w, intra, q_dec, k_dec, g_last))
    state0 = jnp.zeros((b, h, dk, dv), jnp.float32)
    _, out = lax.scan(step, state0, xs)
    return from_chunks(out)


def chunk_hgrn2(q, k, v, log_f):
    b, s, h, dk = q.shape
    dv = v.shape[-1]
    c = C_CHUNK
    q, k, v, log_f = (to_chunks(t, c) for t in (q, k, v, log_f))
    cum = jnp.cumsum(log_f, axis=-2)
    cum_last = cum[..., -1, :]
    q_dec = q * jnp.exp(cum)
    k_dec = k * jnp.exp(cum_last[..., None, :] - cum)
    causal = jnp.tril(jnp.ones((c, c), bool))[:, :, None]

    def step(state, xs):
        q_c, k_c, cum_c, qd_c, kd_c, v_c, cl_c = xs
        rel = masked_exp(cum_c[..., :, None, :] - cum_c[..., None, :, :], causal)
        scores = jnp.einsum('bhid,bhjd,bhijd->bhij', q_c, k_c, rel)
        out = (jnp.einsum('bhid,bhde->bhie', qd_c, state)
               + jnp.einsum('bhij,bhje->bhie', scores, v_c))
        state = jnp.exp(cl_c)[..., None] * state + jnp.einsum('bhjd,bhje->bhde', kd_c, v_c)
        return state, out

    xs = tuple(jnp.moveaxis(t, 2, 0) for t in (q, k, cum, q_dec, k_dec, v, cum_last))
    state0 = jnp.zeros((b, h, dk, dv), jnp.float32)
    _, out = lax.scan(step, state0, xs)
    return from_chunks(out)


def mla_branch(cq, ckv, kr, za, q_norm, w_uq, kv_norm, w_ukv, cos, sin):
    b, s, _ = cq.shape
    qa = (rms_norm(cq, q_norm) @ w_uq).reshape(b, s, A_HEADS, A_NOPE + A_ROPE)
    q_nope = qa[..., :A_NOPE]
    q_rope = apply_rope(qa[..., A_NOPE:], cos[:, :, None, :], sin[:, :, None, :])
    kv = (rms_norm(ckv, kv_norm) @ w_ukv).reshape(b, s, A_HEADS, A_NOPE + A_VDIM)
    k_nope, v = kv[..., :A_NOPE], kv[..., A_NOPE:]
    k_rope = apply_rope(kr, cos, sin)
    o = causal_block_attention(q_nope, q_rope, k_nope, k_rope, v)
    return o.reshape(b, s, BRANCH_W) * jax.nn.silu(za)


def gdn_branch(qkv, ba, bb, zb, conv_w, a_log, dt_bias, norm_g):
    b, s, _ = qkv.shape
    qkv = jax.nn.silu(causal_depthwise_conv(qkv, conv_w)).astype(jnp.float32)
    q, k, v = (t.reshape(b, s, B_HEADS, -1) for t in jnp.split(qkv, 3, axis=-1))
    q, k = l2_norm(q), l2_norm(k)
    g = -jnp.exp(a_log.astype(jnp.float32)) * jax.nn.softplus(
        ba.astype(jnp.float32) + dt_bias.astype(jnp.float32))
    beta = jax.nn.sigmoid(bb.astype(jnp.float32))
    o = chunk_gated_delta_rule(q, k, v, g, beta)
    o = rms_norm(o, norm_g).astype(zb.dtype)
    return o.reshape(b, s, BRANCH_W) * jax.nn.silu(zb)


def hgrn2_branch(cq, cf, ci, zc, lower_bound, norm_g):
    b, s, _ = cq.shape
    q = (jax.nn.silu(cq.astype(jnp.float32)) * C_DK ** -0.5).reshape(b, s, C_HEADS, C_DK)
    z = cf.astype(jnp.float32).reshape(b, s, C_HEADS, C_DK)
    lb = lower_bound.astype(jnp.float32).reshape(C_HEADS, C_DK)
    sig = jax.nn.sigmoid(z)
    f = lb + (1.0 - lb) * sig
    log_f = jnp.log(jnp.maximum(f, 1e-30))
    k = (1.0 - lb) * (1.0 - sig)
    v = ci.astype(jnp.float32).reshape(b, s, C_HEADS, C_DV)
    o = chunk_hgrn2(q, k, v, log_f)
    o = rms_norm(o, norm_g).astype(zc.dtype)
    return o.reshape(b, s, BRANCH_W) * jax.nn.silu(zc)


def hgrn_lower_bounds(logits):
    pr = jax.nn.softmax(logits.astype(jnp.float32), axis=0)
    return jnp.clip(jnp.cumsum(pr, axis=0) - pr[0:1], 0.0, 1.0 - 1e-6)


def setup_inputs(seed: int = 0) -> dict:
    key = jax.random.key(seed)
    ks = jax.random.split(key, 24)
    f32 = jnp.float32
    L, D = DEPTH, D_MODEL

    def normal(k, shape, scale):
        return jax.random.normal(k, shape, f32) * scale

    def gain(k, shape):
        return 1.0 + 0.01 * jax.random.normal(k, shape, f32)

    x = jax.random.normal(ks[0], (BATCH, SEQ, D), f32)
    p = jax.random.normal(ks[1], (DEPTH, BATCH, SEQ, PLE_DIM), f32)
    offsets = jax.random.randint(ks[2], (BATCH, 1), 0, 4096, dtype=jnp.int32)
    positions = offsets + jnp.arange(SEQ, dtype=jnp.int32)[None, :]
    w_in = normal(ks[3], (L, D, IN_WIDTH), D ** -0.5)
    a_q_norm = gain(ks[4], (L, A_Q_LORA))
    a_w_uq = normal(ks[5], (L, A_Q_LORA, A_HEADS * (A_NOPE + A_ROPE)), A_Q_LORA ** -0.5)
    a_kv_norm = gain(ks[6], (L, A_KV_LORA))
    a_w_ukv = normal(ks[7], (L, A_KV_LORA, A_HEADS * (A_NOPE + A_VDIM)), A_KV_LORA ** -0.5)
    b_conv = normal(ks[8], (L, B_CONV, 1, 3 * B_HEADS * B_DK), B_CONV ** -0.5)
    b_a_log = jnp.log(jax.random.uniform(ks[9], (L, B_HEADS), f32, 1.0, 16.0))
    dt = jnp.exp(jax.random.uniform(ks[10], (L, B_HEADS), f32, math.log(1e-3), math.log(1e-1)))
    b_dt_bias = dt + jnp.log(-jnp.expm1(-dt))
    b_norm = gain(ks[11], (L, B_DV))
    c_lb_logits = normal(ks[12], (L, C_HEADS * C_DK), 1.0)
    c_norm = gain(ks[13], (L, C_DV))
    w_branch = normal(ks[14], (L, N_BRANCH, BRANCH_W, D), BRANCH_W ** -0.5 * DEEPNORM_BETA)
    w_out = normal(ks[15], (L, D, D), D ** -0.5 * DEEPNORM_BETA)
    ple_proj = normal(ks[16], (L, PLE_DIM, D), PLE_DIM ** -0.5)
    ple_gate = normal(ks[17], (L, D, D), D ** -0.5)
    ln_g = gain(ks[18], (L, D))
    ln_b = normal(ks[19], (L, D), 0.01)
    return {'x': x, 'p': p, 'positions': positions, 'w_in': w_in,
            'a_q_norm': a_q_norm, 'a_w_uq': a_w_uq, 'a_kv_norm': a_kv_norm, 'a_w_ukv': a_w_ukv,
            'b_conv': b_conv, 'b_a_log': b_a_log, 'b_dt_bias': b_dt_bias, 'b_norm': b_norm,
            'c_lb_logits': c_lb_logits, 'c_norm': c_norm, 'w_branch': w_branch, 'w_out': w_out,
            'ple_proj': ple_proj, 'ple_gate': ple_gate, 'ln_g': ln_g, 'ln_b': ln_b}


def reference(x, p, positions, w_in, a_q_norm, a_w_uq, a_kv_norm, a_w_ukv, b_conv, b_a_log,
              b_dt_bias, b_norm, c_lb_logits, c_norm, w_branch, w_out, ple_proj, ple_gate,
              ln_g, ln_b):
    b, s, _ = x.shape
    cos, sin = rope_tables(positions)
    lower_bounds = hgrn_lower_bounds(c_lb_logits)
    for i in range(DEPTH):
        h = x @ w_in[i]
        (cq, ckv, kr, za, qkv, ba, bb, zb, hq, hf, hi, zc, gate_logits) = jnp.split(
            h, IN_OFFSETS, axis=-1)
        y_a = mla_branch(cq, ckv, kr, za, a_q_norm[i], a_w_uq[i], a_kv_norm[i], a_w_ukv[i],
                         cos, sin)
        y_b = gdn_branch(qkv, ba, bb, zb, b_conv[i], b_a_log[i], b_dt_bias[i], b_norm[i])
        y_c = hgrn2_branch(hq, hf, hi, zc, lower_bounds[i], c_norm[i])
        branches = jnp.stack([y_a, y_b, y_c], axis=2)
        proj = jnp.einsum('bsnw,nwd->bsnd', branches, w_branch[i])
        gates = jax.nn.sigmoid(gate_logits).reshape(b, s, N_BRANCH, D_MODEL)
        merged = jnp.einsum('bsnd,bsnd->bsd', gates, proj)
        r = DEEPNORM_ALPHA * x + merged @ w_out[i]
        r = r + jax.nn.sigmoid(r @ ple_gate[i]) * (p[i] @ ple_proj[i])
        x = layer_norm(r, ln_g[i], ln_b[i])
    return x
```

```python
import functools

import numpy as np
import jax
import jax.numpy as jnp
from jax import lax
from jax.experimental import pallas as pl
from jax.experimental.pallas import tpu as pltpu

F32 = jnp.float32
BF16 = jnp.bfloat16
HIGHEST = lax.Precision.HIGHEST

D_MODEL = 1024
DEPTH = 4
PLE_DIM = 256
BRANCH_W = 512
N_BRANCH = 3
A_HEADS = 8
A_NOPE = 64
A_ROPE = 32
A_VDIM = 64
A_Q_LORA = 256
A_KV_LORA = 128
A_SCALE = (A_NOPE + A_ROPE) ** -0.5
ROPE_THETA = 10000.0
B_HEADS = 4
B_DK = 128
B_DV = 128
B_CONV = 4
B_CHUNK = 64
C_HEADS = 4
C_DK = 128
C_DV = 128
NORM_EPS = 1e-6
MASK_VALUE = -1e30
DEEPNORM_ALPHA = (2.0 * DEPTH) ** 0.25

IN_SPLITS = (A_Q_LORA, A_KV_LORA, A_ROPE, BRANCH_W, 3 * B_HEADS * B_DK, B_HEADS, B_HEADS, BRANCH_W,
             C_HEADS * C_DK, C_HEADS * C_DK, C_HEADS * C_DV, BRANCH_W, N_BRANCH * D_MODEL)
IN_OFFSETS = tuple(int(o) for o in np.cumsum((0,) + IN_SPLITS))

LANES = 128
SUBLANES = 8
A_HEAD_PAD = LANES
VMEM_LIMIT = 56 * 1024 * 1024

TOKEN_TILE = 512
ATTN_TILE = 512
REC_TILE = 256
HGRN_SUB = SUBLANES


def _sigmoid(x):
    return 1.0 / (1.0 + jnp.exp(-x))


def _silu(x):
    return x * _sigmoid(x)


def _softplus(x):
    return jnp.maximum(x, 0.0) + jnp.log(1.0 + jnp.exp(-jnp.abs(x)))


def _rms(x, g):
    return x * lax.rsqrt(jnp.mean(x * x, axis=-1, keepdims=True) + NORM_EPS) * g


def _dot(a, b):
    return jnp.dot(a, b, preferred_element_type=F32)


def _dot_nt(a, b):
    return lax.dot_general(a, b, (((1,), (1,)), ((), ())), preferred_element_type=F32)


def _dot_tn(a, b):
    return lax.dot_general(a, b, (((0,), (0,)), ((), ())), preferred_element_type=F32)


def _dot_hi(a, b):
    return jnp.dot(a, b, preferred_element_type=F32, precision=HIGHEST)


def _const_spec(shape):
    nd = len(shape)
    return pl.BlockSpec(shape, lambda *_: (0,) * nd, pipeline_mode=pl.Buffered(1))


def _params(*sem):
    return pltpu.CompilerParams(dimension_semantics=sem, vmem_limit_bytes=VMEM_LIMIT)


def _mla_prep_kernel(xb_ref, cos_ref, sin_ref, wa_ref, qg_ref, kvg_ref, wq_ref, wqr_ref, wk_ref, wv_ref,
                     q_ref, k_ref, v_ref):
    h = _dot(xb_ref[...], wa_ref[...])
    cq = h[:, :A_Q_LORA]
    ckv = h[:, A_Q_LORA:A_Q_LORA + A_KV_LORA]
    kr = h[:, A_Q_LORA + A_KV_LORA:A_Q_LORA + A_KV_LORA + LANES]
    kr_rot = h[:, A_Q_LORA + A_KV_LORA + LANES:]
    cqn = _rms(cq, qg_ref[...]).astype(BF16)
    ckvn = _rms(ckv, kvg_ref[...]).astype(BF16)
    cos = cos_ref[...]
    sin = sin_ref[...]
    lane = lax.broadcasted_iota(jnp.int32, cos.shape, 1)
    cq_tab = jnp.where(lane < A_NOPE, 1.0, cos) * A_SCALE
    sq_tab = sin * A_SCALE
    cq_tab = jnp.concatenate([cq_tab] * A_HEADS, axis=1)
    sq_tab = jnp.concatenate([sq_tab] * A_HEADS, axis=1)
    q = _dot(cqn, wq_ref[...]) * cq_tab + _dot(cqn, wqr_ref[...]) * sq_tab
    q_ref[...] = q.astype(BF16)
    k_rope = kr * cos + kr_rot * sin
    k = _dot(ckvn, wk_ref[...]) + jnp.concatenate([k_rope] * A_HEADS, axis=1)
    k_ref[...] = k.astype(BF16)
    v_ref[...] = _dot(ckvn, wv_ref[...]).astype(BF16)


def _mla_prep(xb, cos_t, sin_t, wa, qg, kvg, wq, wqr, wk, wv):
    t = xb.shape[0]
    tm = min(TOKEN_TILE, t)
    row = lambda i: (i, 0)
    hp = A_HEADS * A_HEAD_PAD
    return pl.pallas_call(
        _mla_prep_kernel,
        grid=(t // tm,),
        in_specs=[pl.BlockSpec((tm, D_MODEL), row), pl.BlockSpec((tm, LANES), row), pl.BlockSpec((tm, LANES), row),
                  _const_spec(wa.shape), _const_spec(qg.shape), _const_spec(kvg.shape), _const_spec(wq.shape),
                  _const_spec(wqr.shape), _const_spec(wk.shape), _const_spec(wv.shape)],
        out_specs=[pl.BlockSpec((tm, hp), row), pl.BlockSpec((tm, hp), row), pl.BlockSpec((tm, BRANCH_W), row)],
        out_shape=[jax.ShapeDtypeStruct((t, hp), BF16), jax.ShapeDtypeStruct((t, hp), BF16),
                   jax.ShapeDtypeStruct((t, BRANCH_W), BF16)],
        compiler_params=_params("parallel"),
        name="mla_prep",
    )(xb, cos_t, sin_t, wa, qg, kvg, wq, wqr, wk, wv)


def _flash_kernel(q_ref, k_ref, v_ref, o_ref, *, blk):
    qi = pl.program_id(2)
    qs = (q_ref[:, :A_HEAD_PAD], q_ref[:, A_HEAD_PAD:])
    row = lax.broadcasted_iota(jnp.int32, (blk, blk), 0)
    col = lax.broadcasted_iota(jnp.int32, (blk, blk), 1)

    def step(j, carry, masked):
        off = pl.multiple_of(j * blk, blk)
        kk = k_ref[pl.ds(off, blk), :]
        vv = v_ref[pl.ds(off, blk), :]
        out = []
        for hh in range(2):
            m, l, acc = carry[3 * hh:3 * hh + 3]
            s = _dot_nt(qs[hh], kk[:, hh * A_HEAD_PAD:(hh + 1) * A_HEAD_PAD])
            if masked:
                s = jnp.where(col <= row, s, MASK_VALUE)
            m_new = jnp.maximum(m, jnp.max(s, axis=1, keepdims=True))
            p = jnp.exp(s - m_new)
            alpha = jnp.exp(m - m_new)
            l = alpha * l + jnp.sum(p, axis=1, keepdims=True)
            acc = alpha * acc + _dot(p.astype(BF16), vv)
            out += [m_new, l, acc]
        return tuple(out)

    init = (jnp.full((blk, 1), -jnp.inf, F32), jnp.zeros((blk, 1), F32), jnp.zeros((blk, 2 * A_VDIM), F32)) * 2
    carry = lax.fori_loop(0, qi, lambda j, c: step(j, c, False), init)
    _, l0, acc0, _, l1, acc1 = step(qi, carry, True)
    lane = lax.broadcasted_iota(jnp.int32, acc0.shape, 1)
    o_ref[...] = jnp.where(lane < A_VDIM, acc0 / l0, acc1 / l1)


def _mla_flash(q, k, v, batch, seq):
    blk = min(ATTN_TILE, seq)
    nq = seq // blk
    return pl.pallas_call(
        functools.partial(_flash_kernel, blk=blk),
        grid=(batch, A_HEADS // 2, nq),
        in_specs=[pl.BlockSpec((blk, 2 * A_HEAD_PAD), lambda b, h, i: (b * nq + i, h)),
                  pl.BlockSpec((seq, 2 * A_HEAD_PAD), lambda b, h, i: (b, h)),
                  pl.BlockSpec((seq, 2 * A_VDIM), lambda b, h, i: (b, h))],
        out_specs=pl.BlockSpec((blk, 2 * A_VDIM), lambda b, h, i: (b * nq + i, h)),
        out_shape=jax.ShapeDtypeStruct((batch * seq, BRANCH_W), F32),
        compiler_params=_params("parallel", "parallel", "arbitrary"),
        name="mla_flash",
    )(q, k, v)


def _cumsum_rows(x, period):
    n = x.shape[0]
    pos = lax.broadcasted_iota(jnp.int32, x.shape, 0) % period
    s = 1
    while s < period:
        x = x + jnp.where(pos >= s, pltpu.roll(x, s, 0), 0.0)
        s *= 2
    return x


def _unit_lower_inverse(low):
    c = low.shape[0]
    eye = (lax.broadcasted_iota(jnp.int32, (c, c), 0) == lax.broadcasted_iota(jnp.int32, (c, c), 1)).astype(F32)
    inv = eye - low
    pw = low
    k = 2
    while k < c:
        pw = _dot_hi(pw, pw)
        inv = inv + _dot_hi(inv, pw)
        k *= 2
    return inv


def _gdn_kernel(xb_ref, w_ref, convw_ref, alog_ref, dtb_ref, ng_ref, o_ref,
                state_ref, xbuf_ref, q_s, k_s, v_s, g_s, beta_s, *, tile):
    c = B_CHUNK
    hd = B_HEADS * B_DK
    first = pl.program_id(1) == 0

    @pl.when(first)
    def _():
        state_ref[...] = jnp.zeros_like(state_ref)
        xbuf_ref[0:SUBLANES, :] = jnp.zeros((SUBLANES, 3 * hd), F32)

    h = _dot(xb_ref[...], w_ref[...])
    xbuf_ref[SUBLANES:, :] = h[:, :3 * hd]
    xfull = xbuf_ref[...]
    conv = xfull[SUBLANES:, :] * convw_ref[B_CONV - 1:B_CONV, :]
    for s in range(1, B_CONV):
        conv = conv + pltpu.roll(xfull, s, 0)[SUBLANES:, :] * convw_ref[B_CONV - 1 - s:B_CONV - s, :]
    xbuf_ref[0:SUBLANES, :] = h[tile - SUBLANES:, :3 * hd]
    qkv = _silu(conv)

    gates = h[:, 3 * hd:]
    g_all = -jnp.exp(alog_ref[...]) * _softplus(gates + dtb_ref[...])
    g_s[...] = _cumsum_rows(g_all, c)
    beta_s[...] = _sigmoid(gates)

    for hh in range(B_HEADS):
        sl = slice(hh * B_DK, (hh + 1) * B_DK)
        qh = qkv[:, hh * B_DK:(hh + 1) * B_DK]
        kh = qkv[:, hd + hh * B_DK:hd + (hh + 1) * B_DK]
        q_s[:, sl] = qh * lax.rsqrt(jnp.sum(qh * qh, axis=-1, keepdims=True) + NORM_EPS) * (B_DK ** -0.5)
        k_s[:, sl] = kh * lax.rsqrt(jnp.sum(kh * kh, axis=-1, keepdims=True) + NORM_EPS)
    v_s[...] = qkv[:, 2 * hd:]

    ri = lax.broadcasted_iota(jnp.int32, (c, c), 0)
    ci = lax.broadcasted_iota(jnp.int32, (c, c), 1)
    tril = ci <= ri
    strict = ci < ri

    def chunk(n, _):
        r0 = pl.multiple_of(n * c, c)
        rows = pl.ds(r0, c)
        for hh in range(B_HEADS):
            sl = slice(hh * B_DK, (hh + 1) * B_DK)
            q = q_s[rows, sl]
            k = k_s[rows, sl]
            v = v_s[rows, sl]
            gc = g_s[rows, hh:hh + 1]
            beta = beta_s[rows, B_HEADS + hh:B_HEADS + hh + 1]
            gcb = jnp.broadcast_to(gc, (c, c))
            diff = gcb - gcb.T
            decay = jnp.where(tril, jnp.exp(jnp.where(tril, diff, 0.0)), 0.0)
            kb16 = k.astype(BF16)
            k_beta = k * beta
            low = jnp.where(strict, _dot_nt(k_beta.astype(BF16), kb16) * decay, 0.0)
            inv = _unit_lower_inverse(low)
            eg = jnp.exp(gc)
            rhs = jnp.concatenate([v * beta, k_beta * eg], axis=1)
            sol = _dot_hi(inv, rhs)
            u = sol[:, :B_DV]
            w = sol[:, B_DV:]
            intra = _dot_nt(q.astype(BF16), kb16) * decay
            state = state_ref[hh]
            st16 = state.astype(BF16)
            v_new = u - _dot(w.astype(BF16), st16)
            vn16 = v_new.astype(BF16)
            out = _dot((q * eg).astype(BF16), st16) + _dot(intra.astype(BF16), vn16)
            g_last = g_s[pl.ds(r0 + c - 1, 1), hh:hh + 1]
            k_dec = k * jnp.exp(g_last - gc)
            state_ref[hh] = state * jnp.exp(g_last) + _dot_tn(k_dec.astype(BF16), vn16)
            o_ref[rows, sl] = _rms(out, ng_ref[...])
        return 0

    lax.fori_loop(0, tile // c, chunk, 0)


def _gdn(xb, w, conv_w, a_log, dt_bias, norm_g, batch, seq):
    tile = min(REC_TILE, seq)
    nt = seq // tile
    hd = B_HEADS * B_DK
    return pl.pallas_call(
        functools.partial(_gdn_kernel, tile=tile),
        grid=(batch, nt),
        in_specs=[pl.BlockSpec((tile, D_MODEL), lambda b, i: (b * nt + i, 0)),
                  _const_spec(w.shape), _const_spec(conv_w.shape), _const_spec(a_log.shape),
                  _const_spec(dt_bias.shape), _const_spec(norm_g.shape)],
        out_specs=pl.BlockSpec((tile, BRANCH_W), lambda b, i: (b * nt + i, 0)),
        out_shape=jax.ShapeDtypeStruct((batch * seq, BRANCH_W), F32),
        scratch_shapes=[pltpu.VMEM((B_HEADS, B_DK, B_DV), F32),
                        pltpu.VMEM((SUBLANES + tile, 3 * hd), F32),
                        pltpu.VMEM((tile, hd), F32), pltpu.VMEM((tile, hd), F32), pltpu.VMEM((tile, hd), F32),
                        pltpu.VMEM((tile, LANES), F32), pltpu.VMEM((tile, LANES), F32)],
        compiler_params=_params("arbitrary", "arbitrary"),
        name="gdn",
    )(xb, w, conv_w, a_log, dt_bias, norm_g)


def _hgrn_levels(tile):
    out = []
    lv = tile // 2
    while lv >= HGRN_SUB:
        out.append(lv)
        lv //= 2
    return tuple(out)


def _hgrn_level_map(tile):
    i = np.arange(tile)[:, None]
    j = np.arange(tile)[None, :]
    x = np.bitwise_xor(i, j)
    lev = np.where(j < i, np.floor(np.log2(np.maximum(x, 1))).astype(np.int32), -1)
    return jnp.asarray(lev, jnp.int32)


def _hgrn_kernel(xb_ref, w_ref, lb_ref, ng_ref, lev_ref, o_ref,
                 state_ref, q_s, k_s, v_s, cum_s, acc_s, *, tile):
    hd = C_HEADS * C_DK

    @pl.when(pl.program_id(1) == 0)
    def _():
        state_ref[...] = jnp.zeros_like(state_ref)

    h = _dot(xb_ref[...], w_ref[...])
    lev = lev_ref[...]
    sub_row = lax.broadcasted_iota(jnp.int32, (HGRN_SUB, C_DV), 0)

    for hh in range(C_HEADS):
        sl = slice(hh * C_DK, (hh + 1) * C_DK)
        lb = lb_ref[:, sl]
        q = _silu(h[:, hh * C_DK:(hh + 1) * C_DK]) * (C_DK ** -0.5)
        sig = _sigmoid(h[:, hd + hh * C_DK:hd + (hh + 1) * C_DK])
        f = lb + (1.0 - lb) * sig
        log_f = jnp.log(jnp.maximum(f, 1e-30))
        k = (1.0 - lb) * (1.0 - sig)
        v = h[:, 2 * hd + hh * C_DV:2 * hd + (hh + 1) * C_DV]
        cum = _cumsum_rows(log_f, tile)

        state_t = state_ref[hh]
        out = _dot_nt((q * jnp.exp(cum)).astype(BF16), state_t.astype(BF16))
        cum_last = cum[tile - 1:tile, :]
        k_dec = k * jnp.exp(cum_last - cum)
        v16 = v.astype(BF16)
        state_ref[hh] = state_t * jnp.exp(cum_last) + _dot_tn(v16, k_dec.astype(BF16))

        scores = jnp.zeros((tile, tile), F32)
        for lv in _hgrn_levels(tile):
            blocks = tile // (2 * lv)
            cum3 = cum.reshape(blocks, 2 * lv, C_DK)
            ref = jnp.broadcast_to(cum3[:, lv:lv + 1, :], cum3.shape).reshape(tile, C_DK)
            qt = q * jnp.exp(jnp.minimum(cum - ref, 0.0))
            kt = k * jnp.exp(jnp.minimum(ref - cum, 0.0))
            s_lv = _dot_nt(qt.astype(BF16), kt.astype(BF16))
            scores = jnp.where(lev == int(np.log2(lv)), s_lv, scores)
        acc_s[...] = out + _dot(scores.astype(BF16), v16)

        q_s[...] = q
        k_s[...] = k
        v_s[...] = v
        cum_s[...] = cum

        def sub_block(s, _):
            r0 = pl.multiple_of(s * HGRN_SUB, HGRN_SUB)
            rows = pl.ds(r0, HGRN_SUB)
            qv = q_s[rows, :]
            cv = cum_s[rows, :]
            acc = acc_s[rows, :]
            for j in range(HGRN_SUB):
                kj = k_s[pl.ds(r0 + j, 1), :]
                cj = cum_s[pl.ds(r0 + j, 1), :]
                vj = v_s[pl.ds(r0 + j, 1), :]
                e = jnp.exp(jnp.minimum(cv - cj, 0.0))
                sc = jnp.sum(qv * kj * e, axis=1, keepdims=True)
                acc = acc + jnp.where(sub_row >= j, sc, 0.0) * vj
            o_ref[rows, sl] = _rms(acc, ng_ref[...])
            return 0

        lax.fori_loop(0, tile // HGRN_SUB, sub_block, 0)


def _hgrn(xb, w, lower_bound, norm_g, batch, seq):
    tile = min(REC_TILE, seq)
    nt = seq // tile
    lev = _hgrn_level_map(tile)
    return pl.pallas_call(
        functools.partial(_hgrn_kernel, tile=tile),
        grid=(batch, nt),
        in_specs=[pl.BlockSpec((tile, D_MODEL), lambda b, i: (b * nt + i, 0)),
                  _const_spec(w.shape), _const_spec(lower_bound.shape), _const_spec(norm_g.shape),
                  _const_spec(lev.shape)],
        out_specs=pl.BlockSpec((tile, BRANCH_W), lambda b, i: (b * nt + i, 0)),
        out_shape=jax.ShapeDtypeStruct((batch * seq, BRANCH_W), F32),
        scratch_shapes=[pltpu.VMEM((C_HEADS, C_DV, C_DK), F32)]
                       + [pltpu.VMEM((tile, C_DK), F32) for _ in range(5)],
        compiler_params=_params("arbitrary", "arbitrary"),
        name="hgrn2",
    )(xb, w, lower_bound, norm_g, lev)


def _merge_kernel(x_ref, xb_ref, oa_ref, ob_ref, oc_ref, p_ref, wz_ref, wg_ref, wbr_ref, wout_ref,
                  pgate_ref, pproj_ref, lng_ref, lnb_ref, xo_ref, xbo_ref):
    xb = xb_ref[...]
    z = _dot(xb, wz_ref[...])
    merged = None
    for n, o_ref in enumerate((oa_ref, ob_ref, oc_ref)):
        y = o_ref[...] * _silu(z[:, n * BRANCH_W:(n + 1) * BRANCH_W])
        proj = _dot(y.astype(BF16), wbr_ref[n])
        gate = _sigmoid(_dot(xb, wg_ref[:, n * D_MODEL:(n + 1) * D_MODEL]))
        merged = gate * proj if merged is None else merged + gate * proj
    r = DEEPNORM_ALPHA * x_ref[...] + _dot(merged.astype(BF16), wout_ref[...])
    ple = _dot(p_ref[...].astype(BF16), pproj_ref[...])
    r = r + _sigmoid(_dot(r.astype(BF16), pgate_ref[...])) * ple
    mu = jnp.mean(r, axis=-1, keepdims=True)
    var = jnp.mean(jnp.square(r - mu), axis=-1, keepdims=True)
    xn = (r - mu) * lax.rsqrt(var + NORM_EPS) * lng_ref[...] + lnb_ref[...]
    xo_ref[...] = xn
    xbo_ref[...] = xn.astype(BF16)


def _merge(x, xb, oa, ob, oc, p, wz, wg, wbr, wout, pgate, pproj, lng, lnb):
    t = x.shape[0]
    tm = min(TOKEN_TILE, t)
    row = lambda i: (i, 0)
    return pl.pallas_call(
        _merge_kernel,
        grid=(t // tm,),
        in_specs=[pl.BlockSpec((tm, D_MODEL), row), pl.BlockSpec((tm, D_MODEL), row),
                  pl.BlockSpec((tm, BRANCH_W), row), pl.BlockSpec((tm, BRANCH_W), row),
                  pl.BlockSpec((tm, BRANCH_W), row), pl.BlockSpec((tm, PLE_DIM), row),
                  _const_spec(wz.shape), _const_spec(wg.shape), _const_spec(wbr.shape), _const_spec(wout.shape),
                  _const_spec(pgate.shape), _const_spec(pproj.shape), _const_spec(lng.shape),
                  _const_spec(lnb.shape)],
        out_specs=[pl.BlockSpec((tm, D_MODEL), row), pl.BlockSpec((tm, D_MODEL), row)],
        out_shape=[jax.ShapeDtypeStruct((t, D_MODEL), F32), jax.ShapeDtypeStruct((t, D_MODEL), BF16)],
        compiler_params=_params("parallel"),
        name="merge",
    )(x, xb, oa, ob, oc, p, wz, wg, wbr, wout, pgate, pproj, lng, lnb)


def _cols(w, idx):
    return w[:, IN_OFFSETS[idx]:IN_OFFSETS[idx + 1]]


def _rot_cols(w):
    half = w.shape[-1] // 2
    return jnp.concatenate([-w[..., half:], w[..., :half]], axis=-1)


def _pad_cols(w, left, total):
    return jnp.pad(w, ((0, 0), (left, total - left - w.shape[-1])))


def _mla_weights(w_in, w_uq, w_ukv):
    w_kr = _cols(w_in, 2)
    wa = jnp.concatenate([_cols(w_in, 0), _cols(w_in, 1), _pad_cols(w_kr, A_NOPE, LANES),
                          _pad_cols(_rot_cols(w_kr), A_NOPE, LANES)], axis=1).astype(BF16)
    uq = w_uq.reshape(A_Q_LORA, A_HEADS, A_NOPE + A_ROPE)
    pad = A_HEAD_PAD - A_NOPE - A_ROPE
    wq = jnp.pad(uq, ((0, 0), (0, 0), (0, pad))).reshape(A_Q_LORA, A_HEADS * A_HEAD_PAD).astype(BF16)
    uq_rot = jnp.concatenate([jnp.zeros_like(uq[..., :A_NOPE]), _rot_cols(uq[..., A_NOPE:])], axis=-1)
    wqr = jnp.pad(uq_rot, ((0, 0), (0, 0), (0, pad))).reshape(A_Q_LORA, A_HEADS * A_HEAD_PAD).astype(BF16)
    ukv = w_ukv.reshape(A_KV_LORA, A_HEADS, A_NOPE + A_VDIM)
    wk = jnp.pad(ukv[..., :A_NOPE], ((0, 0), (0, 0), (0, A_HEAD_PAD - A_NOPE)))
    wk = wk.reshape(A_KV_LORA, A_HEADS * A_HEAD_PAD).astype(BF16)
    wv = ukv[..., A_NOPE:].reshape(A_KV_LORA, A_HEADS * A_VDIM).astype(BF16)
    return wa, wq, wqr, wk, wv


def _rope_tables(positions):
    inv = ROPE_THETA ** (-jnp.arange(0, A_ROPE, 2, dtype=F32) / A_ROPE)
    ang = positions.astype(F32).reshape(-1, 1) * inv
    place = lambda t: jnp.pad(jnp.concatenate([t, t], axis=-1), ((0, 0), (A_NOPE, LANES - A_NOPE - A_ROPE)))
    return place(jnp.cos(ang)), place(jnp.sin(ang))


def _lower_bounds(logits):
    pr = jax.nn.softmax(logits.astype(F32), axis=0)
    return jnp.clip(jnp.cumsum(pr, axis=0) - pr[0:1], 0.0, 1.0 - 1e-6)


def _lane_row(v):
    return jnp.pad(v.astype(F32), (0, LANES - v.shape[0])).reshape(1, LANES)


def kernel(x, p, positions, w_in, a_q_norm, a_w_uq, a_kv_norm, a_w_ukv, b_conv, b_a_log, b_dt_bias, b_norm,
           c_lb_logits, c_norm, w_branch, w_out, ple_proj, ple_gate, ln_g, ln_b):
    batch, seq, _ = x.shape
    t = batch * seq
    cos_t, sin_t = _rope_tables(positions)
    lower_bounds = _lower_bounds(c_lb_logits)
    xf = x.reshape(t, D_MODEL)
    xb = xf.astype(BF16)
    for i in range(DEPTH):
        wi = w_in[i]
        wa, wq, wqr, wk, wv = _mla_weights(wi, a_w_uq[i], a_w_ukv[i])
        q, k, v = _mla_prep(xb, cos_t, sin_t, wa, a_q_norm[i].reshape(1, -1), a_kv_norm[i].reshape(1, -1),
                            wq, wqr, wk, wv)
        o_a = _mla_flash(q, k, v, batch, seq)

        gate_cols = jnp.concatenate([_cols(wi, 5), _cols(wi, 6)], axis=1)
        wb = jnp.concatenate([_cols(wi, 4), _pad_cols(gate_cols, 0, LANES)], axis=1).astype(BF16)
        o_b = _gdn(xb, wb, b_conv[i].reshape(B_CONV, -1), _lane_row(b_a_log[i]), _lane_row(b_dt_bias[i]),
                   b_norm[i].reshape(1, -1), batch, seq)

        wc = jnp.concatenate([_cols(wi, 8), _cols(wi, 9), _cols(wi, 10)], axis=1).astype(BF16)
        o_c = _hgrn(xb, wc, lower_bounds[i].reshape(1, -1), c_norm[i].reshape(1, -1), batch, seq)

        wz = jnp.concatenate([_cols(wi, 3), _cols(wi, 7), _cols(wi, 11)], axis=1).astype(BF16)
        xf, xb = _merge(xf, xb, o_a, o_b, o_c, p[i].reshape(t, PLE_DIM), wz, _cols(wi, 12).astype(BF16),
                        w_branch[i].astype(BF16), w_out[i].astype(BF16), ple_gate[i].astype(BF16),
                        ple_proj[i].astype(BF16), ln_g[i].reshape(1, -1), ln_b[i].reshape(1, -1))
    return xf.reshape(batch, seq, D_MODEL)
```

```python
import functools

import numpy as np
import jax
import jax.numpy as jnp
from jax import lax
from jax.experimental import pallas as pl
from jax.experimental.pallas import tpu as pltpu

F32 = jnp.float32
BF16 = jnp.bfloat16
HIGHEST = lax.Precision.HIGHEST

D_MODEL = 1024
DEPTH = 4
PLE_DIM = 256
BRANCH_W = 512
N_BRANCH = 3
A_HEADS = 8
A_NOPE = 64
A_ROPE = 32
A_VDIM = 64
A_Q_LORA = 256
A_KV_LORA = 128
A_SCALE = (A_NOPE + A_ROPE) ** -0.5
ROPE_THETA = 10000.0
B_HEADS = 4
B_DK = 128
B_DV = 128
B_CONV = 4
B_CHUNK = 64
C_HEADS = 4
C_DK = 128
C_DV = 128
NORM_EPS = 1e-6
MASK_VALUE = -1e30
DEEPNORM_ALPHA = (2.0 * DEPTH) ** 0.25
LOG2_E = 1.4426950408889634

IN_SPLITS = (A_Q_LORA, A_KV_LORA, A_ROPE, BRANCH_W, 3 * B_HEADS * B_DK, B_HEADS, B_HEADS, BRANCH_W,
             C_HEADS * C_DK, C_HEADS * C_DK, C_HEADS * C_DV, BRANCH_W, N_BRANCH * D_MODEL)
IN_OFFSETS = tuple(int(o) for o in np.cumsum((0,) + IN_SPLITS))

LANES = 128
SUBLANES = 8
A_HEAD_PAD = LANES
VMEM_LIMIT = 56 * 1024 * 1024

TOKEN_TILE = 512
ATTN_TILE = 512
ATTN_GROUP = 4
REC_TILE = 256
HGRN_SUB = SUBLANES
HGRN_GROUP = 8


def _sigmoid(x):
    return 1.0 / (1.0 + jnp.exp(-x))


def _silu(x):
    return x * _sigmoid(x)


def _softplus(x):
    return jnp.maximum(x, 0.0) + jnp.log(1.0 + jnp.exp(-jnp.abs(x)))


def _rms(x, g):
    return x * lax.rsqrt(jnp.mean(x * x, axis=-1, keepdims=True) + NORM_EPS) * g


def _dot(a, b):
    return jnp.dot(a, b, preferred_element_type=F32)


def _dot_nt(a, b):
    return lax.dot_general(a, b, (((1,), (1,)), ((), ())), preferred_element_type=F32)


def _dot_tn(a, b):
    return lax.dot_general(a, b, (((0,), (0,)), ((), ())), preferred_element_type=F32)


def _dot_hi(a, b):
    return jnp.dot(a, b, preferred_element_type=F32, precision=HIGHEST)


def _const_spec(shape):
    nd = len(shape)
    return pl.BlockSpec(shape, lambda *_: (0,) * nd, pipeline_mode=pl.Buffered(1))


def _params(*sem):
    return pltpu.CompilerParams(dimension_semantics=sem, vmem_limit_bytes=VMEM_LIMIT)


def _mla_prep_kernel(xb_ref, cos_ref, sin_ref, wa_ref, qg_ref, kvg_ref, wq_ref, wqr_ref, wk_ref, wv_ref,
                     q_ref, k_ref, v_ref):
    h = _dot(xb_ref[...], wa_ref[...])
    cq = h[:, :A_Q_LORA]
    ckv = h[:, A_Q_LORA:A_Q_LORA + A_KV_LORA]
    kr = h[:, A_Q_LORA + A_KV_LORA:A_Q_LORA + A_KV_LORA + LANES]
    kr_rot = h[:, A_Q_LORA + A_KV_LORA + LANES:]
    cqn = _rms(cq, qg_ref[...]).astype(BF16)
    ckvn = _rms(ckv, kvg_ref[...]).astype(BF16)
    cos = cos_ref[...]
    sin = sin_ref[...]
    lane = lax.broadcasted_iota(jnp.int32, cos.shape, 1)
    cq_tab = jnp.where(lane < A_NOPE, 1.0, cos) * (A_SCALE * LOG2_E)
    sq_tab = sin * (A_SCALE * LOG2_E)
    cq_tab = jnp.concatenate([cq_tab] * A_HEADS, axis=1)
    sq_tab = jnp.concatenate([sq_tab] * A_HEADS, axis=1)
    q = _dot(cqn, wq_ref[...]) * cq_tab + _dot(cqn, wqr_ref[...]) * sq_tab
    q_ref[...] = q.astype(BF16)
    k_rope = kr * cos + kr_rot * sin
    k = _dot(ckvn, wk_ref[...]) + jnp.concatenate([k_rope] * A_HEADS, axis=1)
    k_ref[...] = k.astype(BF16)
    v = _dot(ckvn, wv_ref[...])
    vlane = lax.broadcasted_iota(jnp.int32, v.shape, 1) % A_HEAD_PAD
    v_ref[...] = jnp.where(vlane == A_VDIM, 1.0, v).astype(BF16)


def _mla_prep(xb, cos_t, sin_t, wa, qg, kvg, wq, wqr, wk, wv):
    t = xb.shape[0]
    tm = min(TOKEN_TILE, t)
    row = lambda i: (i, 0)
    hp = A_HEADS * A_HEAD_PAD
    return pl.pallas_call(
        _mla_prep_kernel,
        grid=(t // tm,),
        in_specs=[pl.BlockSpec((tm, D_MODEL), row), pl.BlockSpec((tm, LANES), row), pl.BlockSpec((tm, LANES), row),
                  _const_spec(wa.shape), _const_spec(qg.shape), _const_spec(kvg.shape), _const_spec(wq.shape),
                  _const_spec(wqr.shape), _const_spec(wk.shape), _const_spec(wv.shape)],
        out_specs=[pl.BlockSpec((tm, hp), row), pl.BlockSpec((tm, hp), row), pl.BlockSpec((tm, hp), row)],
        out_shape=[jax.ShapeDtypeStruct((t, hp), BF16)] * 3,
        compiler_params=_params("parallel"),
        name="mla_prep",
    )(xb, cos_t, sin_t, wa, qg, kvg, wq, wqr, wk, wv)


def _flash_kernel(q_ref, k_ref, v_ref, o_ref, *, blk):
    qi = pl.program_id(2)
    qs = (q_ref[:, :A_HEAD_PAD], q_ref[:, A_HEAD_PAD:])
    heads = tuple(slice(hh * A_HEAD_PAD, (hh + 1) * A_HEAD_PAD) for hh in range(2))
    row = lax.broadcasted_iota(jnp.int32, (blk, blk), 0)
    col = lax.broadcasted_iota(jnp.int32, (blk, blk), 1)

    def scores(j):
        kk = k_ref[pl.ds(pl.multiple_of(j * blk, blk), blk), :]
        return [_dot_nt(qs[hh], kk[:, heads[hh]]) for hh in range(2)]

    def update(carry, j, s, masked):
        vv = v_ref[pl.ds(pl.multiple_of(j * blk, blk), blk), :]
        out = []
        for hh in range(2):
            m, acc = carry[2 * hh:2 * hh + 2]
            sh = jnp.where(col <= row, s[hh], MASK_VALUE) if masked else s[hh]
            m_new = jnp.maximum(m, jnp.max(sh, axis=1, keepdims=True))
            p = jnp.exp2(sh - m_new)
            acc = jnp.exp2(m - m_new) * acc + _dot(p.astype(BF16), vv[:, heads[hh]])
            out += [m_new, acc]
        return tuple(out)

    def group(i, carry):
        js = [ATTN_GROUP * i + t for t in range(ATTN_GROUP)]
        ss = [scores(j) for j in js]
        for j, s in zip(js, ss):
            carry = update(carry, j, s, False)
        return carry

    init = (jnp.full((blk, 1), -jnp.inf, F32), jnp.zeros((blk, A_HEAD_PAD), F32)) * 2
    groups = qi // ATTN_GROUP
    carry = lax.fori_loop(0, groups, group, init)
    carry = lax.fori_loop(groups * ATTN_GROUP, qi, lambda j, c: update(c, j, scores(j), False), carry)
    _, acc0, _, acc1 = update(carry, qi, scores(qi), True)
    o0 = acc0 / acc0[:, A_VDIM:A_VDIM + 1]
    o1 = acc1 / acc1[:, A_VDIM:A_VDIM + 1]
    lane = lax.broadcasted_iota(jnp.int32, o0.shape, 1)
    o_ref[...] = jnp.where(lane < A_VDIM, o0, pltpu.roll(o1, A_VDIM, 1))


def _mla_flash(q, k, v, batch, seq):
    blk = min(ATTN_TILE, seq)
    nq = seq // blk
    return pl.pallas_call(
        functools.partial(_flash_kernel, blk=blk),
        grid=(batch, A_HEADS // 2, nq),
        in_specs=[pl.BlockSpec((blk, 2 * A_HEAD_PAD), lambda b, h, i: (b * nq + i, h)),
                  pl.BlockSpec((seq, 2 * A_HEAD_PAD), lambda b, h, i: (b, h)),
                  pl.BlockSpec((seq, 2 * A_HEAD_PAD), lambda b, h, i: (b, h))],
        out_specs=pl.BlockSpec((blk, 2 * A_VDIM), lambda b, h, i: (b * nq + i, h)),
        out_shape=jax.ShapeDtypeStruct((batch * seq, BRANCH_W), F32),
        compiler_params=_params("parallel", "parallel", "arbitrary"),
        name="mla_flash",
    )(q, k, v)


def _cumsum_rows(x, period):
    n = x.shape[0]
    pos = lax.broadcasted_iota(jnp.int32, x.shape, 0) % period
    s = 1
    while s < period:
        x = x + jnp.where(pos >= s, pltpu.roll(x, s, 0), 0.0)
        s *= 2
    return x


def _bdot(a, b):
    return lax.dot_general(a, b, (((2,), (1,)), ((0,), (0,))), preferred_element_type=F32)


def _bdot_nt(a, b):
    return lax.dot_general(a, b, (((2,), (2,)), ((0,), (0,))), preferred_element_type=F32)


def _unit_lower_inverse(low):
    c = low.shape[-1]
    eye = (lax.broadcasted_iota(jnp.int32, (c, c), 0) == lax.broadcasted_iota(jnp.int32, (c, c), 1)).astype(F32)
    inv = eye - low
    pw16 = low.astype(BF16)
    k = 2
    while k < c:
        pw16 = _bdot(pw16, pw16).astype(BF16)
        inv = inv + _bdot(inv.astype(BF16), pw16)
        k *= 2
    return inv


def _gdn_kernel(xb_ref, w_ref, convw_ref, alog_ref, dtb_ref, ng_ref, o_ref,
                state_ref, xbuf_ref, g_s, u_s, w_s, qd_s, kd_s, intra_s, out_s, *, tile):
    c = B_CHUNK
    n = tile // c
    hd = B_HEADS * B_DK

    @pl.when(pl.program_id(1) == 0)
    def _():
        state_ref[...] = jnp.zeros_like(state_ref)
        xbuf_ref[0:SUBLANES, :] = jnp.zeros((SUBLANES, 3 * hd), F32)

    h = _dot(xb_ref[...], w_ref[...])
    xbuf_ref[SUBLANES:, :] = h[:, :3 * hd]
    xfull = xbuf_ref[...]
    conv = xfull[SUBLANES:, :] * convw_ref[B_CONV - 1:B_CONV, :]
    for s in range(1, B_CONV):
        conv = conv + pltpu.roll(xfull, s, 0)[SUBLANES:, :] * convw_ref[B_CONV - 1 - s:B_CONV - s, :]
    xbuf_ref[0:SUBLANES, :] = h[tile - SUBLANES:, :3 * hd]
    qkv = _silu(conv)

    gates = h[:, 3 * hd:]
    g_all = -jnp.exp(alog_ref[...]) * _softplus(gates + dtb_ref[...])
    gcum = _cumsum_rows(g_all, c)
    g_s[...] = gcum
    beta_all = _sigmoid(gates)

    ri = lax.broadcasted_iota(jnp.int32, (c, c), 0)
    ci = lax.broadcasted_iota(jnp.int32, (c, c), 1)
    tril = ci <= ri
    strict = ci < ri

    for hh in range(B_HEADS):
        sl = slice(hh * B_DK, (hh + 1) * B_DK)
        qh = qkv[:, hh * B_DK:(hh + 1) * B_DK]
        kh = qkv[:, hd + hh * B_DK:hd + (hh + 1) * B_DK]
        q = (qh * lax.rsqrt(jnp.sum(qh * qh, axis=-1, keepdims=True) + NORM_EPS) * (B_DK ** -0.5)).reshape(n, c, B_DK)
        k = (kh * lax.rsqrt(jnp.sum(kh * kh, axis=-1, keepdims=True) + NORM_EPS)).reshape(n, c, B_DK)
        v = qkv[:, 2 * hd + hh * B_DV:2 * hd + (hh + 1) * B_DV].reshape(n, c, B_DV)
        gc = gcum[:, hh:hh + 1].reshape(n, c, 1)
        beta = beta_all[:, B_HEADS + hh:B_HEADS + hh + 1].reshape(n, c, 1)
        gcb = jnp.broadcast_to(gc, (n, c, c))
        diff = gcb - jnp.swapaxes(gcb, 1, 2)
        decay = jnp.where(tril, jnp.exp(jnp.where(tril, diff, 0.0)), 0.0)
        k16 = k.astype(BF16)
        k_beta = k * beta
        low = jnp.where(strict, _bdot_nt(k_beta.astype(BF16), k16) * decay, 0.0)
        inv = _unit_lower_inverse(low)
        eg = jnp.exp(gc)
        rhs = jnp.concatenate([v * beta, k_beta * eg], axis=2)
        sol = _bdot(inv.astype(BF16), rhs.astype(BF16))
        u_s[:, sl] = sol[:, :, :B_DV].reshape(tile, B_DV)
        w_s[:, sl] = sol[:, :, B_DV:].reshape(tile, B_DK).astype(BF16)
        intra_s[hh] = (_bdot_nt(q.astype(BF16), k16) * decay).reshape(tile, c).astype(BF16)
        qd_s[:, sl] = (q * eg).reshape(tile, B_DK).astype(BF16)
        g_last = gc[:, c - 1:c, :]
        kd_s[:, sl] = (k * jnp.exp(g_last - gc)).reshape(tile, B_DK).astype(BF16)

    heads = range(B_HEADS)
    cols = [slice(hh * B_DK, (hh + 1) * B_DK) for hh in heads]
    for nn in range(n):
        rows = slice(nn * c, (nn + 1) * c)
        st = [state_ref[hh] for hh in heads]
        st16 = [s.astype(BF16) for s in st]
        v_new = [u_s[rows, cols[hh]] - _dot(w_s[rows, cols[hh]], st16[hh]) for hh in heads]
        vn16 = [x.astype(BF16) for x in v_new]
        for hh in heads:
            out_s[rows, cols[hh]] = _dot(qd_s[rows, cols[hh]], st16[hh]) + _dot(intra_s[hh, rows, :], vn16[hh])
        for hh in heads:
            g_last = g_s[(nn + 1) * c - 1:(nn + 1) * c, hh:hh + 1]
            state_ref[hh] = st[hh] * jnp.exp(g_last) + _dot_tn(kd_s[rows, cols[hh]], vn16[hh])

    for hh in heads:
        o_ref[:, cols[hh]] = _rms(out_s[:, cols[hh]], ng_ref[...])


def _gdn(xb, w, conv_w, a_log, dt_bias, norm_g, batch, seq):
    tile = min(REC_TILE, seq)
    nt = seq // tile
    hd = B_HEADS * B_DK
    return pl.pallas_call(
        functools.partial(_gdn_kernel, tile=tile),
        grid=(batch, nt),
        in_specs=[pl.BlockSpec((tile, D_MODEL), lambda b, i: (b * nt + i, 0)),
                  _const_spec(w.shape), _const_spec(conv_w.shape), _const_spec(a_log.shape),
                  _const_spec(dt_bias.shape), _const_spec(norm_g.shape)],
        out_specs=pl.BlockSpec((tile, BRANCH_W), lambda b, i: (b * nt + i, 0)),
        out_shape=jax.ShapeDtypeStruct((batch * seq, BRANCH_W), F32),
        scratch_shapes=[pltpu.VMEM((B_HEADS, B_DK, B_DV), F32),
                        pltpu.VMEM((SUBLANES + tile, 3 * hd), F32),
                        pltpu.VMEM((tile, LANES), F32),
                        pltpu.VMEM((tile, hd), F32),
                        pltpu.VMEM((tile, hd), BF16),
                        pltpu.VMEM((tile, hd), BF16),
                        pltpu.VMEM((tile, hd), BF16),
                        pltpu.VMEM((B_HEADS, tile, B_CHUNK), BF16),
                        pltpu.VMEM((tile, hd), F32)],
        compiler_params=_params("arbitrary", "arbitrary"),
        name="gdn",
    )(xb, w, conv_w, a_log, dt_bias, norm_g)


def _hgrn_levels(tile):
    out = []
    lv = tile // 2
    while lv >= HGRN_SUB:
        out.append(lv)
        lv //= 2
    return tuple(out)


def _hgrn_level_map(tile):
    i = np.arange(tile)[:, None]
    j = np.arange(tile)[None, :]
    x = np.bitwise_xor(i, j)
    lev = np.where(j < i, np.floor(np.log2(np.maximum(x, 1))).astype(np.int32), -1)
    return jnp.asarray(lev, jnp.int32)


def _hgrn_kernel(xb_ref, w_ref, lb_ref, ng_ref, lev_ref, o_ref,
                 state_ref, q_s, k_s, v_s, cum_s, acc_s, *, tile):
    hd = C_HEADS * C_DK

    @pl.when(pl.program_id(1) == 0)
    def _():
        state_ref[...] = jnp.zeros_like(state_ref)

    h = _dot(xb_ref[...], w_ref[...])
    lev = lev_ref[...]
    sub_row = lax.broadcasted_iota(jnp.int32, (HGRN_SUB, C_DV), 0)

    for hh in range(C_HEADS):
        sl = slice(hh * C_DK, (hh + 1) * C_DK)
        lb = lb_ref[:, sl]
        q = _silu(h[:, hh * C_DK:(hh + 1) * C_DK]) * (C_DK ** -0.5)
        sig = _sigmoid(h[:, hd + hh * C_DK:hd + (hh + 1) * C_DK])
        f = lb + (1.0 - lb) * sig
        log_f = jnp.log(jnp.maximum(f, 1e-30))
        k = (1.0 - lb) * (1.0 - sig)
        v = h[:, 2 * hd + hh * C_DV:2 * hd + (hh + 1) * C_DV]
        cum = _cumsum_rows(log_f, tile)

        state_t = state_ref[hh]
        out = _dot_nt((q * jnp.exp(cum)).astype(BF16), state_t.astype(BF16))
        cum_last = cum[tile - 1:tile, :]
        k_dec = k * jnp.exp(cum_last - cum)
        v16 = v.astype(BF16)
        state_ref[hh] = state_t * jnp.exp(cum_last) + _dot_tn(v16, k_dec.astype(BF16))

        scores = jnp.zeros((tile, tile), F32)
        for lv in _hgrn_levels(tile):
            blocks = tile // (2 * lv)
            cum3 = cum.reshape(blocks, 2 * lv, C_DK)
            ref = jnp.broadcast_to(cum3[:, lv:lv + 1, :], cum3.shape).reshape(tile, C_DK)
            qt = q * jnp.exp(jnp.minimum(cum - ref, 0.0))
            kt = k * jnp.exp(jnp.minimum(ref - cum, 0.0))
            s_lv = _dot_nt(qt.astype(BF16), kt.astype(BF16))
            scores = jnp.where(lev == int(np.log2(lv)), s_lv, scores)
        acc_s[...] = out + _dot(scores.astype(BF16), v16)

        q_s[...] = q
        k_s[...] = k
        v_s[...] = v
        cum_s[...] = cum

        def group(gi, _):
            base = pl.multiple_of(gi * (HGRN_GROUP * HGRN_SUB), HGRN_GROUP * HGRN_SUB)
            blocks = [pl.ds(base + ss * HGRN_SUB, HGRN_SUB) for ss in range(HGRN_GROUP)]
            qv = [q_s[r, :] for r in blocks]
            cv = [cum_s[r, :] for r in blocks]
            acc = [acc_s[r, :] for r in blocks]
            for j in range(HGRN_SUB):
                for ss in range(HGRN_GROUP):
                    rj = pl.ds(base + ss * HGRN_SUB + j, 1)
                    e = jnp.exp(jnp.minimum(cv[ss] - cum_s[rj, :], 0.0))
                    sc = jnp.sum(qv[ss] * k_s[rj, :] * e, axis=1, keepdims=True)
                    acc[ss] = acc[ss] + jnp.where(sub_row >= j, sc, 0.0) * v_s[rj, :]
            for ss in range(HGRN_GROUP):
                acc_s[blocks[ss], :] = acc[ss]
            return 0

        lax.fori_loop(0, tile // (HGRN_GROUP * HGRN_SUB), group, 0)
        o_ref[:, sl] = _rms(acc_s[...], ng_ref[...])


def _hgrn(xb, w, lower_bound, norm_g, batch, seq):
    tile = min(REC_TILE, seq)
    nt = seq // tile
    lev = _hgrn_level_map(tile)
    return pl.pallas_call(
        functools.partial(_hgrn_kernel, tile=tile),
        grid=(batch, nt),
        in_specs=[pl.BlockSpec((tile, D_MODEL), lambda b, i: (b * nt + i, 0)),
                  _const_spec(w.shape), _const_spec(lower_bound.shape), _const_spec(norm_g.shape),
                  _const_spec(lev.shape)],
        out_specs=pl.BlockSpec((tile, BRANCH_W), lambda b, i: (b * nt + i, 0)),
        out_shape=jax.ShapeDtypeStruct((batch * seq, BRANCH_W), F32),
        scratch_shapes=[pltpu.VMEM((C_HEADS, C_DV, C_DK), F32)]
                       + [pltpu.VMEM((tile, C_DK), F32) for _ in range(5)],
        compiler_params=_params("arbitrary", "arbitrary"),
        name="hgrn2",
    )(xb, w, lower_bound, norm_g, lev)


def _merge_kernel(x_ref, xb_ref, oa_ref, ob_ref, oc_ref, p_ref, wz_ref, wg_ref, wbr_ref, wout_ref,
                  pgate_ref, pproj_ref, lng_ref, lnb_ref, xo_ref, xbo_ref):
    xb = xb_ref[...]
    z = _dot(xb, wz_ref[...])
    merged = None
    for n, o_ref in enumerate((oa_ref, ob_ref, oc_ref)):
        y = o_ref[...] * _silu(z[:, n * BRANCH_W:(n + 1) * BRANCH_W])
        proj = _dot(y.astype(BF16), wbr_ref[n])
        gate = _sigmoid(_dot(xb, wg_ref[:, n * D_MODEL:(n + 1) * D_MODEL]))
        merged = gate * proj if merged is None else merged + gate * proj
    r = DEEPNORM_ALPHA * x_ref[...] + _dot(merged.astype(BF16), wout_ref[...])
    ple = _dot(p_ref[...].astype(BF16), pproj_ref[...])
    r = r + _sigmoid(_dot(r.astype(BF16), pgate_ref[...])) * ple
    mu = jnp.mean(r, axis=-1, keepdims=True)
    var = jnp.mean(jnp.square(r - mu), axis=-1, keepdims=True)
    xn = (r - mu) * lax.rsqrt(var + NORM_EPS) * lng_ref[...] + lnb_ref[...]
    xo_ref[...] = xn
    xbo_ref[...] = xn.astype(BF16)


def _merge(x, xb, oa, ob, oc, p, wz, wg, wbr, wout, pgate, pproj, lng, lnb):
    t = x.shape[0]
    tm = min(TOKEN_TILE, t)
    row = lambda i: (i, 0)
    return pl.pallas_call(
        _merge_kernel,
        grid=(t // tm,),
        in_specs=[pl.BlockSpec((tm, D_MODEL), row), pl.BlockSpec((tm, D_MODEL), row),
                  pl.BlockSpec((tm, BRANCH_W), row), pl.BlockSpec((tm, BRANCH_W), row),
                  pl.BlockSpec((tm, BRANCH_W), row), pl.BlockSpec((tm, PLE_DIM), row),
                  _const_spec(wz.shape), _const_spec(wg.shape), _const_spec(wbr.shape), _const_spec(wout.shape),
                  _const_spec(pgate.shape), _const_spec(pproj.shape), _const_spec(lng.shape),
                  _const_spec(lnb.shape)],
        out_specs=[pl.BlockSpec((tm, D_MODEL), row), pl.BlockSpec((tm, D_MODEL), row)],
        out_shape=[jax.ShapeDtypeStruct((t, D_MODEL), F32), jax.ShapeDtypeStruct((t, D_MODEL), BF16)],
        compiler_params=_params("parallel"),
        name="merge",
    )(x, xb, oa, ob, oc, p, wz, wg, wbr, wout, pgate, pproj, lng, lnb)


def _cols(w, idx):
    return w[:, IN_OFFSETS[idx]:IN_OFFSETS[idx + 1]]


def _rot_cols(w):
    half = w.shape[-1] // 2
    return jnp.concatenate([-w[..., half:], w[..., :half]], axis=-1)


def _pad_cols(w, left, total):
    return jnp.pad(w, ((0, 0), (left, total - left - w.shape[-1])))


def _mla_weights(w_in, w_uq, w_ukv):
    w_kr = _cols(w_in, 2)
    wa = jnp.concatenate([_cols(w_in, 0), _cols(w_in, 1), _pad_cols(w_kr, A_NOPE, LANES),
                          _pad_cols(_rot_cols(w_kr), A_NOPE, LANES)], axis=1).astype(BF16)
    uq = w_uq.reshape(A_Q_LORA, A_HEADS, A_NOPE + A_ROPE)
    pad = A_HEAD_PAD - A_NOPE - A_ROPE
    wq = jnp.pad(uq, ((0, 0), (0, 0), (0, pad))).reshape(A_Q_LORA, A_HEADS * A_HEAD_PAD).astype(BF16)
    uq_rot = jnp.concatenate([jnp.zeros_like(uq[..., :A_NOPE]), _rot_cols(uq[..., A_NOPE:])], axis=-1)
    wqr = jnp.pad(uq_rot, ((0, 0), (0, 0), (0, pad))).reshape(A_Q_LORA, A_HEADS * A_HEAD_PAD).astype(BF16)
    ukv = w_ukv.reshape(A_KV_LORA, A_HEADS, A_NOPE + A_VDIM)
    wk = jnp.pad(ukv[..., :A_NOPE], ((0, 0), (0, 0), (0, A_HEAD_PAD - A_NOPE)))
    wk = wk.reshape(A_KV_LORA, A_HEADS * A_HEAD_PAD).astype(BF16)
    wv = jnp.pad(ukv[..., A_NOPE:], ((0, 0), (0, 0), (0, A_HEAD_PAD - A_VDIM)))
    wv = wv.reshape(A_KV_LORA, A_HEADS * A_HEAD_PAD).astype(BF16)
    return wa, wq, wqr, wk, wv


def _rope_tables(positions):
    inv = ROPE_THETA ** (-jnp.arange(0, A_ROPE, 2, dtype=F32) / A_ROPE)
    ang = positions.astype(F32).reshape(-1, 1) * inv
    place = lambda t: jnp.pad(jnp.concatenate([t, t], axis=-1), ((0, 0), (A_NOPE, LANES - A_NOPE - A_ROPE)))
    return place(jnp.cos(ang)), place(jnp.sin(ang))


def _lower_bounds(logits):
    pr = jax.nn.softmax(logits.astype(F32), axis=0)
    return jnp.clip(jnp.cumsum(pr, axis=0) - pr[0:1], 0.0, 1.0 - 1e-6)


def _lane_row(v):
    return jnp.pad(v.astype(F32), (0, LANES - v.shape[0])).reshape(1, LANES)


def kernel(x, p, positions, w_in, a_q_norm, a_w_uq, a_kv_norm, a_w_ukv, b_conv, b_a_log, b_dt_bias, b_norm,
           c_lb_logits, c_norm, w_branch, w_out, ple_proj, ple_gate, ln_g, ln_b):
    batch, seq, _ = x.shape
    t = batch * seq
    cos_t, sin_t = _rope_tables(positions)
    lower_bounds = _lower_bounds(c_lb_logits)
    xf = x.reshape(t, D_MODEL)
    xb = xf.astype(BF16)
    for i in range(DEPTH):
        wi = w_in[i]
        wa, wq, wqr, wk, wv = _mla_weights(wi, a_w_uq[i], a_w_ukv[i])
        q, k, v = _mla_prep(xb, cos_t, sin_t, wa, a_q_norm[i].reshape(1, -1), a_kv_norm[i].reshape(1, -1),
                            wq, wqr, wk, wv)
        o_a = _mla_flash(q, k, v, batch, seq)

        gate_cols = jnp.concatenate([_cols(wi, 5), _cols(wi, 6)], axis=1)
        wb = jnp.concatenate([_cols(wi, 4), _pad_cols(gate_cols, 0, LANES)], axis=1).astype(BF16)
        o_b = _gdn(xb, wb, b_conv[i].reshape(B_CONV, -1), _lane_row(b_a_log[i]), _lane_row(b_dt_bias[i]),
                   b_norm[i].reshape(1, -1), batch, seq)

        wc = jnp.concatenate([_cols(wi, 8), _cols(wi, 9), _cols(wi, 10)], axis=1).astype(BF16)
        o_c = _hgrn(xb, wc, lower_bounds[i].reshape(1, -1), c_norm[i].reshape(1, -1), batch, seq)

        wz = jnp.concatenate([_cols(wi, 3), _cols(wi, 7), _cols(wi, 11)], axis=1).astype(BF16)
        xf, xb = _merge(xf, xb, o_a, o_b, o_c, p[i].reshape(t, PLE_DIM), wz, _cols(wi, 12).astype(BF16),
                        w_branch[i].astype(BF16), w_out[i].astype(BF16), ple_gate[i].astype(BF16),
                        ple_proj[i].astype(BF16), ln_g[i].reshape(1, -1), ln_b[i].reshape(1, -1))
    return xf.reshape(batch, seq, D_MODEL)
```

```python
import functools

import numpy as np
import jax
import jax.numpy as jnp
from jax import lax
from jax.experimental import pallas as pl
from jax.experimental.pallas import tpu as pltpu

F32 = jnp.float32
BF16 = jnp.bfloat16
HIGHEST = lax.Precision.HIGHEST

D_MODEL = 1024
DEPTH = 4
PLE_DIM = 256
BRANCH_W = 512
N_BRANCH = 3
A_HEADS = 8
A_NOPE = 64
A_ROPE = 32
A_VDIM = 64
A_Q_LORA = 256
A_KV_LORA = 128
A_SCALE = (A_NOPE + A_ROPE) ** -0.5
ROPE_THETA = 10000.0
B_HEADS = 4
B_DK = 128
B_DV = 128
B_CONV = 4
B_CHUNK = 64
C_HEADS = 4
C_DK = 128
C_DV = 128
NORM_EPS = 1e-6
MASK_VALUE = -1e30
DEEPNORM_ALPHA = (2.0 * DEPTH) ** 0.25
LOG2_E = 1.4426950408889634

IN_SPLITS = (A_Q_LORA, A_KV_LORA, A_ROPE, BRANCH_W, 3 * B_HEADS * B_DK, B_HEADS, B_HEADS, BRANCH_W,
             C_HEADS * C_DK, C_HEADS * C_DK, C_HEADS * C_DV, BRANCH_W, N_BRANCH * D_MODEL)
IN_OFFSETS = tuple(int(o) for o in np.cumsum((0,) + IN_SPLITS))

LANES = 128
SUBLANES = 8
A_HEAD_PAD = LANES
VMEM_LIMIT = 56 * 1024 * 1024

TOKEN_TILE = 512
ATTN_TILE = 512
ATTN_GROUP = 4
REC_TILE = 256
HGRN_SUB = SUBLANES
HGRN_GROUP = 16


def _sigmoid(x):
    return 1.0 / (1.0 + jnp.exp(-x))


def _silu(x):
    return x * _sigmoid(x)


def _softplus(x):
    return jnp.maximum(x, 0.0) + jnp.log(1.0 + jnp.exp(-jnp.abs(x)))


def _rms(x, g):
    return x * lax.rsqrt(jnp.mean(x * x, axis=-1, keepdims=True) + NORM_EPS) * g


def _dot(a, b):
    return jnp.dot(a, b, preferred_element_type=F32)


def _dot_nt(a, b):
    return lax.dot_general(a, b, (((1,), (1,)), ((), ())), preferred_element_type=F32)


def _dot_tn(a, b):
    return lax.dot_general(a, b, (((0,), (0,)), ((), ())), preferred_element_type=F32)


def _dot_hi(a, b):
    return jnp.dot(a, b, preferred_element_type=F32, precision=HIGHEST)


def _const_spec(shape):
    nd = len(shape)
    return pl.BlockSpec(shape, lambda *_: (0,) * nd, pipeline_mode=pl.Buffered(1))


def _params(*sem):
    return pltpu.CompilerParams(dimension_semantics=sem, vmem_limit_bytes=VMEM_LIMIT)


def _mla_prep_kernel(xb_ref, cos_ref, sin_ref, wa_ref, qg_ref, kvg_ref, wq_ref, wqr_ref, wk_ref, wv_ref,
                     q_ref, k_ref, v_ref):
    h = _dot(xb_ref[...], wa_ref[...])
    cq = h[:, :A_Q_LORA]
    ckv = h[:, A_Q_LORA:A_Q_LORA + A_KV_LORA]
    kr = h[:, A_Q_LORA + A_KV_LORA:A_Q_LORA + A_KV_LORA + LANES]
    kr_rot = h[:, A_Q_LORA + A_KV_LORA + LANES:]
    cqn = _rms(cq, qg_ref[...]).astype(BF16)
    ckvn = _rms(ckv, kvg_ref[...]).astype(BF16)
    cos = cos_ref[...]
    sin = sin_ref[...]
    lane = lax.broadcasted_iota(jnp.int32, cos.shape, 1)
    cq_tab = jnp.where(lane < A_NOPE, 1.0, cos) * (A_SCALE * LOG2_E)
    sq_tab = sin * (A_SCALE * LOG2_E)
    cq_tab = jnp.concatenate([cq_tab] * A_HEADS, axis=1)
    sq_tab = jnp.concatenate([sq_tab] * A_HEADS, axis=1)
    q = _dot(cqn, wq_ref[...]) * cq_tab + _dot(cqn, wqr_ref[...]) * sq_tab
    q_ref[...] = q.astype(BF16)
    k_rope = kr * cos + kr_rot * sin
    k = _dot(ckvn, wk_ref[...]) + jnp.concatenate([k_rope] * A_HEADS, axis=1)
    k_ref[...] = k.astype(BF16)
    v = _dot(ckvn, wv_ref[...])
    vlane = lax.broadcasted_iota(jnp.int32, v.shape, 1) % A_HEAD_PAD
    v_ref[...] = jnp.where(vlane == A_VDIM, 1.0, v).astype(BF16)


def _mla_prep(xb, cos_t, sin_t, wa, qg, kvg, wq, wqr, wk, wv):
    t = xb.shape[0]
    tm = min(TOKEN_TILE, t)
    row = lambda i: (i, 0)
    hp = A_HEADS * A_HEAD_PAD
    return pl.pallas_call(
        _mla_prep_kernel,
        grid=(t // tm,),
        in_specs=[pl.BlockSpec((tm, D_MODEL), row), pl.BlockSpec((tm, LANES), row), pl.BlockSpec((tm, LANES), row),
                  _const_spec(wa.shape), _const_spec(qg.shape), _const_spec(kvg.shape), _const_spec(wq.shape),
                  _const_spec(wqr.shape), _const_spec(wk.shape), _const_spec(wv.shape)],
        out_specs=[pl.BlockSpec((tm, hp), row), pl.BlockSpec((tm, hp), row), pl.BlockSpec((tm, hp), row)],
        out_shape=[jax.ShapeDtypeStruct((t, hp), BF16)] * 3,
        compiler_params=_params("parallel"),
        name="mla_prep",
    )(xb, cos_t, sin_t, wa, qg, kvg, wq, wqr, wk, wv)


def _flash_kernel(q_ref, k_ref, v_ref, o_ref, *, blk):
    qi = pl.program_id(2)
    qs = (q_ref[:, :A_HEAD_PAD], q_ref[:, A_HEAD_PAD:])
    heads = tuple(slice(hh * A_HEAD_PAD, (hh + 1) * A_HEAD_PAD) for hh in range(2))
    row = lax.broadcasted_iota(jnp.int32, (blk, blk), 0)
    col = lax.broadcasted_iota(jnp.int32, (blk, blk), 1)

    def scores(j):
        kk = k_ref[pl.ds(pl.multiple_of(j * blk, blk), blk), :]
        return [_dot_nt(qs[hh], kk[:, heads[hh]]) for hh in range(2)]

    def update(carry, j, s, masked):
        vv = v_ref[pl.ds(pl.multiple_of(j * blk, blk), blk), :]
        out = []
        for hh in range(2):
            m, acc = carry[2 * hh:2 * hh + 2]
            sh = jnp.where(col <= row, s[hh], MASK_VALUE) if masked else s[hh]
            m_new = jnp.maximum(m, jnp.max(sh, axis=1, keepdims=True))
            p = jnp.exp2(sh - m_new)
            acc = jnp.exp2(m - m_new) * acc + _dot(p.astype(BF16), vv[:, heads[hh]])
            out += [m_new, acc]
        return tuple(out)

    def group(i, carry):
        js = [ATTN_GROUP * i + t for t in range(ATTN_GROUP)]
        ss = [scores(j) for j in js]
        for j, s in zip(js, ss):
            carry = update(carry, j, s, False)
        return carry

    init = (jnp.full((blk, 1), -jnp.inf, F32), jnp.zeros((blk, A_HEAD_PAD), F32)) * 2
    groups = qi // ATTN_GROUP
    carry = lax.fori_loop(0, groups, group, init)
    carry = lax.fori_loop(groups * ATTN_GROUP, qi, lambda j, c: update(c, j, scores(j), False), carry)
    _, acc0, _, acc1 = update(carry, qi, scores(qi), True)
    o0 = acc0 / acc0[:, A_VDIM:A_VDIM + 1]
    o1 = acc1 / acc1[:, A_VDIM:A_VDIM + 1]
    lane = lax.broadcasted_iota(jnp.int32, o0.shape, 1)
    o_ref[...] = jnp.where(lane < A_VDIM, o0, pltpu.roll(o1, A_VDIM, 1))


def _mla_flash(q, k, v, batch, seq):
    blk = min(ATTN_TILE, seq)
    nq = seq // blk
    return pl.pallas_call(
        functools.partial(_flash_kernel, blk=blk),
        grid=(batch, A_HEADS // 2, nq),
        in_specs=[pl.BlockSpec((blk, 2 * A_HEAD_PAD), lambda b, h, i: (b * nq + i, h)),
                  pl.BlockSpec((seq, 2 * A_HEAD_PAD), lambda b, h, i: (b, h)),
                  pl.BlockSpec((seq, 2 * A_HEAD_PAD), lambda b, h, i: (b, h))],
        out_specs=pl.BlockSpec((blk, 2 * A_VDIM), lambda b, h, i: (b * nq + i, h)),
        out_shape=jax.ShapeDtypeStruct((batch * seq, BRANCH_W), F32),
        compiler_params=_params("parallel", "parallel", "arbitrary"),
        name="mla_flash",
    )(q, k, v)


def _cumsum_rows(x, period):
    n = x.shape[0]
    pos = lax.broadcasted_iota(jnp.int32, x.shape, 0) % period
    s = 1
    while s < period:
        x = x + jnp.where(pos >= s, pltpu.roll(x, s, 0), 0.0)
        s *= 2
    return x


def _bdot(a, b):
    return lax.dot_general(a, b, (((2,), (1,)), ((0,), (0,))), preferred_element_type=F32)


def _bdot_nt(a, b):
    return lax.dot_general(a, b, (((2,), (2,)), ((0,), (0,))), preferred_element_type=F32)


def _unit_lower_inverse(low):
    c = low.shape[-1]
    eye = (lax.broadcasted_iota(jnp.int32, (c, c), 0) == lax.broadcasted_iota(jnp.int32, (c, c), 1)).astype(F32)
    inv = eye - low
    pw16 = low.astype(BF16)
    k = 2
    while k < c:
        pw16 = _bdot(pw16, pw16).astype(BF16)
        inv = inv + _bdot(inv.astype(BF16), pw16)
        k *= 2
    return inv


def _gdn_kernel(xb_ref, w_ref, convw_ref, alog_ref, dtb_ref, ng_ref, o_ref,
                state_ref, xbuf_ref, m_s, n_s, qp_s, o0_s, egl_s, out_s, *, tile):
    c = B_CHUNK
    n = tile // c
    nb = B_HEADS * n
    hd = B_HEADS * B_DK

    @pl.when(pl.program_id(1) == 0)
    def _():
        state_ref[...] = jnp.zeros_like(state_ref)
        xbuf_ref[0:SUBLANES, :] = jnp.zeros((SUBLANES, 3 * hd), F32)

    h = _dot(xb_ref[...], w_ref[...])
    xbuf_ref[SUBLANES:, :] = h[:, :3 * hd]
    xfull = xbuf_ref[...]
    conv = xfull[SUBLANES:, :] * convw_ref[B_CONV - 1:B_CONV, :]
    for s in range(1, B_CONV):
        conv = conv + pltpu.roll(xfull, s, 0)[SUBLANES:, :] * convw_ref[B_CONV - 1 - s:B_CONV - s, :]
    xbuf_ref[0:SUBLANES, :] = h[tile - SUBLANES:, :3 * hd]
    qkv = _silu(conv)

    gates = h[:, 3 * hd:]
    g_all = -jnp.exp(alog_ref[...]) * _softplus(gates + dtb_ref[...])
    gcum = _cumsum_rows(g_all, c)
    beta_all = _sigmoid(gates)

    def per_head(fn):
        return jnp.concatenate([fn(hh) for hh in range(B_HEADS)], axis=0)

    def unit(x):
        return x * lax.rsqrt(jnp.sum(x * x, axis=-1, keepdims=True) + NORM_EPS)

    q = per_head(lambda hh: (unit(qkv[:, hh * B_DK:(hh + 1) * B_DK]) * (B_DK ** -0.5)).reshape(n, c, B_DK))
    k = per_head(lambda hh: unit(qkv[:, hd + hh * B_DK:hd + (hh + 1) * B_DK]).reshape(n, c, B_DK))
    v = per_head(lambda hh: qkv[:, 2 * hd + hh * B_DV:2 * hd + (hh + 1) * B_DV].reshape(n, c, B_DV))
    gc = per_head(lambda hh: gcum[:, hh:hh + 1].reshape(n, c, 1))
    beta = per_head(lambda hh: beta_all[:, B_HEADS + hh:B_HEADS + hh + 1].reshape(n, c, 1))

    ri = lax.broadcasted_iota(jnp.int32, (c, c), 0)
    ci = lax.broadcasted_iota(jnp.int32, (c, c), 1)
    tril = ci <= ri
    gcb = jnp.broadcast_to(gc, (nb, c, c))
    diff = gcb - jnp.swapaxes(gcb, 1, 2)
    decay = jnp.where(tril, jnp.exp(jnp.where(tril, diff, 0.0)), 0.0)
    k16 = k.astype(BF16)
    k_beta = k * beta
    low = jnp.where(ci < ri, _bdot_nt(k_beta.astype(BF16), k16) * decay, 0.0)
    inv = _unit_lower_inverse(low)
    eg = jnp.exp(gc)
    rhs = jnp.concatenate([v * beta, k_beta * eg], axis=2)
    uw16 = _bdot(inv.astype(BF16), rhs.astype(BF16)).astype(BF16)
    intra16 = (_bdot_nt(q.astype(BF16), k16) * decay).astype(BF16)
    g_last = gc[:, c - 1:c, :]
    kd_t = jnp.swapaxes(k * jnp.exp(g_last - gc), 1, 2).astype(BF16)
    kuw = _bdot(kd_t, uw16)
    iuw = _bdot(intra16, uw16)
    n_s[...] = kuw[:, :, :B_DV]
    m_s[...] = kuw[:, :, B_DV:].astype(BF16)
    o0_s[...] = iuw[:, :, :B_DV]
    qp_s[...] = (q * eg - iuw[:, :, B_DV:]).astype(BF16)
    egl_s[...] = jnp.broadcast_to(jnp.exp(g_last), (nb, 1, B_DV))

    heads = range(B_HEADS)
    st = [state_ref[hh] for hh in heads]
    for nn in range(n):
        st16 = [s.astype(BF16) for s in st]
        ms = [_dot(m_s[hh * n + nn], st16[hh]) for hh in heads]
        for hh in heads:
            out_s[nn * c:(nn + 1) * c, hh * B_DV:(hh + 1) * B_DV] = _dot(qp_s[hh * n + nn], st16[hh]) + o0_s[hh * n + nn]
        st = [st[hh] * egl_s[hh * n + nn] - ms[hh] + n_s[hh * n + nn] for hh in heads]
    for hh in heads:
        state_ref[hh] = st[hh]
        o_ref[:, hh * B_DV:(hh + 1) * B_DV] = _rms(out_s[:, hh * B_DV:(hh + 1) * B_DV], ng_ref[...])


def _gdn(xb, w, conv_w, a_log, dt_bias, norm_g, batch, seq):
    tile = min(REC_TILE, seq)
    nt = seq // tile
    hd = B_HEADS * B_DK
    nb = B_HEADS * (tile // B_CHUNK)
    return pl.pallas_call(
        functools.partial(_gdn_kernel, tile=tile),
        grid=(batch, nt),
        in_specs=[pl.BlockSpec((tile, D_MODEL), lambda b, i: (b * nt + i, 0)),
                  _const_spec(w.shape), _const_spec(conv_w.shape), _const_spec(a_log.shape),
                  _const_spec(dt_bias.shape), _const_spec(norm_g.shape)],
        out_specs=pl.BlockSpec((tile, BRANCH_W), lambda b, i: (b * nt + i, 0)),
        out_shape=jax.ShapeDtypeStruct((batch * seq, BRANCH_W), F32),
        scratch_shapes=[pltpu.VMEM((B_HEADS, B_DK, B_DV), F32),
                        pltpu.VMEM((SUBLANES + tile, 3 * hd), F32),
                        pltpu.VMEM((nb, B_DK, B_DK), BF16),
                        pltpu.VMEM((nb, B_DK, B_DV), F32),
                        pltpu.VMEM((nb, B_CHUNK, B_DK), BF16),
                        pltpu.VMEM((nb, B_CHUNK, B_DV), F32),
                        pltpu.VMEM((nb, 1, B_DV), F32),
                        pltpu.VMEM((tile, hd), F32)],
        compiler_params=_params("arbitrary", "arbitrary"),
        name="gdn",
    )(xb, w, conv_w, a_log, dt_bias, norm_g)


def _hgrn_levels(tile):
    out = []
    lv = tile // 2
    while lv >= HGRN_SUB:
        out.append(lv)
        lv //= 2
    return tuple(out)


def _hgrn_level_map(tile):
    i = np.arange(tile)[:, None]
    j = np.arange(tile)[None, :]
    x = np.bitwise_xor(i, j)
    lev = np.where(j < i, np.floor(np.log2(np.maximum(x, 1))).astype(np.int32), -1)
    return jnp.asarray(lev, jnp.int32)


def _hgrn_kernel(xb_ref, w_ref, lb_ref, ng_ref, lev_ref, o_ref,
                 state_ref, q_s, k_s, v_s, cum_s, acc_s, *, tile):
    hd = C_HEADS * C_DK

    @pl.when(pl.program_id(1) == 0)
    def _():
        state_ref[...] = jnp.zeros_like(state_ref)

    h = _dot(xb_ref[...], w_ref[...])
    lev = lev_ref[...]
    sub_row = lax.broadcasted_iota(jnp.int32, (HGRN_SUB, C_DV), 0)

    for hh in range(C_HEADS):
        sl = slice(hh * C_DK, (hh + 1) * C_DK)
        lb = lb_ref[:, sl]
        q = _silu(h[:, hh * C_DK:(hh + 1) * C_DK]) * (C_DK ** -0.5)
        sig = _sigmoid(h[:, hd + hh * C_DK:hd + (hh + 1) * C_DK])
        f = lb + (1.0 - lb) * sig
        log_f = jnp.log(jnp.maximum(f, 1e-30))
        k = (1.0 - lb) * (1.0 - sig)
        v = h[:, 2 * hd + hh * C_DV:2 * hd + (hh + 1) * C_DV]
        cum = _cumsum_rows(log_f, tile)

        state_t = state_ref[hh]
        out = _dot_nt((q * jnp.exp(cum)).astype(BF16), state_t.astype(BF16))
        cum_last = cum[tile - 1:tile, :]
        k_dec = k * jnp.exp(cum_last - cum)
        v16 = v.astype(BF16)
        state_ref[hh] = state_t * jnp.exp(cum_last) + _dot_tn(v16, k_dec.astype(BF16))

        scores = jnp.zeros((tile, tile), F32)
        for lv in _hgrn_levels(tile):
            blocks = tile // (2 * lv)
            cum3 = cum.reshape(blocks, 2 * lv, C_DK)
            ref = jnp.broadcast_to(cum3[:, lv:lv + 1, :], cum3.shape).reshape(tile, C_DK)
            qt = q * jnp.exp(jnp.minimum(cum - ref, 0.0))
            kt = k * jnp.exp(jnp.minimum(ref - cum, 0.0))
            s_lv = _dot_nt(qt.astype(BF16), kt.astype(BF16))
            scores = jnp.where(lev == int(np.log2(lv)), s_lv, scores)
        acc_s[...] = out + _dot(scores.astype(BF16), v16)

        q_s[...] = q
        k_s[...] = k
        v_s[...] = v
        cum_s[...] = cum

        def group(gi, _):
            base = pl.multiple_of(gi * (HGRN_GROUP * HGRN_SUB), HGRN_GROUP * HGRN_SUB)
            blocks = [pl.ds(base + ss * HGRN_SUB, HGRN_SUB) for ss in range(HGRN_GROUP)]
            qv = [q_s[r, :] for r in blocks]
            cv = [cum_s[r, :] for r in blocks]
            acc = [acc_s[r, :] for r in blocks]
            for j in range(HGRN_SUB):
                for ss in range(HGRN_GROUP):
                    rj = pl.ds(base + ss * HGRN_SUB + j, 1)
                    e = jnp.exp(jnp.minimum(cv[ss] - cum_s[rj, :], 0.0))
                    sc = jnp.sum(qv[ss] * k_s[rj, :] * e, axis=1, keepdims=True)
                    acc[ss] = acc[ss] + jnp.where(sub_row >= j, sc, 0.0) * v_s[rj, :]
            for ss in range(HGRN_GROUP):
                acc_s[blocks[ss], :] = acc[ss]
            return 0

        lax.fori_loop(0, tile // (HGRN_GROUP * HGRN_SUB), group, 0)
        o_ref[:, sl] = _rms(acc_s[...], ng_ref[...])


def _hgrn(xb, w, lower_bound, norm_g, batch, seq):
    tile = min(REC_TILE, seq)
    nt = seq // tile
    lev = _hgrn_level_map(tile)
    return pl.pallas_call(
        functools.partial(_hgrn_kernel, tile=tile),
        grid=(batch, nt),
        in_specs=[pl.BlockSpec((tile, D_MODEL), lambda b, i: (b * nt + i, 0)),
                  _const_spec(w.shape), _const_spec(lower_bound.shape), _const_spec(norm_g.shape),
                  _const_spec(lev.shape)],
        out_specs=pl.BlockSpec((tile, BRANCH_W), lambda b, i: (b * nt + i, 0)),
        out_shape=jax.ShapeDtypeStruct((batch * seq, BRANCH_W), F32),
        scratch_shapes=[pltpu.VMEM((C_HEADS, C_DV, C_DK), F32)]
                       + [pltpu.VMEM((tile, C_DK), F32) for _ in range(5)],
        compiler_params=_params("arbitrary", "arbitrary"),
        name="hgrn2",
    )(xb, w, lower_bound, norm_g, lev)


def _merge_kernel(x_ref, xb_ref, oa_ref, ob_ref, oc_ref, p_ref, wz_ref, wg_ref, wbr_ref, wout_ref,
                  pgate_ref, pproj_ref, lng_ref, lnb_ref, xo_ref, xbo_ref):
    xb = xb_ref[...]
    z = _dot(xb, wz_ref[...])
    merged = None
    for n, o_ref in enumerate((oa_ref, ob_ref, oc_ref)):
        y = o_ref[...] * _silu(z[:, n * BRANCH_W:(n + 1) * BRANCH_W])
        proj = _dot(y.astype(BF16), wbr_ref[n])
        gate = _sigmoid(_dot(xb, wg_ref[:, n * D_MODEL:(n + 1) * D_MODEL]))
        merged = gate * proj if merged is None else merged + gate * proj
    r = DEEPNORM_ALPHA * x_ref[...] + _dot(merged.astype(BF16), wout_ref[...])
    ple = _dot(p_ref[...].astype(BF16), pproj_ref[...])
    r = r + _sigmoid(_dot(r.astype(BF16), pgate_ref[...])) * ple
    mu = jnp.mean(r, axis=-1, keepdims=True)
    var = jnp.mean(jnp.square(r - mu), axis=-1, keepdims=True)
    xn = (r - mu) * lax.rsqrt(var + NORM_EPS) * lng_ref[...] + lnb_ref[...]
    xo_ref[...] = xn
    xbo_ref[...] = xn.astype(BF16)


def _merge(x, xb, oa, ob, oc, p, wz, wg, wbr, wout, pgate, pproj, lng, lnb):
    t = x.shape[0]
    tm = min(TOKEN_TILE, t)
    row = lambda i: (i, 0)
    return pl.pallas_call(
        _merge_kernel,
        grid=(t // tm,),
        in_specs=[pl.BlockSpec((tm, D_MODEL), row), pl.BlockSpec((tm, D_MODEL), row),
                  pl.BlockSpec((tm, BRANCH_W), row), pl.BlockSpec((tm, BRANCH_W), row),
                  pl.BlockSpec((tm, BRANCH_W), row), pl.BlockSpec((tm, PLE_DIM), row),
                  _const_spec(wz.shape), _const_spec(wg.shape), _const_spec(wbr.shape), _const_spec(wout.shape),
                  _const_spec(pgate.shape), _const_spec(pproj.shape), _const_spec(lng.shape),
                  _const_spec(lnb.shape)],
        out_specs=[pl.BlockSpec((tm, D_MODEL), row), pl.BlockSpec((tm, D_MODEL), row)],
        out_shape=[jax.ShapeDtypeStruct((t, D_MODEL), F32), jax.ShapeDtypeStruct((t, D_MODEL), BF16)],
        compiler_params=_params("parallel"),
        name="merge",
    )(x, xb, oa, ob, oc, p, wz, wg, wbr, wout, pgate, pproj, lng, lnb)


def _cols(w, idx):
    return w[:, IN_OFFSETS[idx]:IN_OFFSETS[idx + 1]]


def _rot_cols(w):
    half = w.shape[-1] // 2
    return jnp.concatenate([-w[..., half:], w[..., :half]], axis=-1)


def _pad_cols(w, left, total):
    return jnp.pad(w, ((0, 0), (left, total - left - w.shape[-1])))


def _mla_weights(w_in, w_uq, w_ukv):
    w_kr = _cols(w_in, 2)
    wa = jnp.concatenate([_cols(w_in, 0), _cols(w_in, 1), _pad_cols(w_kr, A_NOPE, LANES),
                          _pad_cols(_rot_cols(w_kr), A_NOPE, LANES)], axis=1).astype(BF16)
    uq = w_uq.reshape(A_Q_LORA, A_HEADS, A_NOPE + A_ROPE)
    pad = A_HEAD_PAD - A_NOPE - A_ROPE
    wq = jnp.pad(uq, ((0, 0), (0, 0), (0, pad))).reshape(A_Q_LORA, A_HEADS * A_HEAD_PAD).astype(BF16)
    uq_rot = jnp.concatenate([jnp.zeros_like(uq[..., :A_NOPE]), _rot_cols(uq[..., A_NOPE:])], axis=-1)
    wqr = jnp.pad(uq_rot, ((0, 0), (0, 0), (0, pad))).reshape(A_Q_LORA, A_HEADS * A_HEAD_PAD).astype(BF16)
    ukv = w_ukv.reshape(A_KV_LORA, A_HEADS, A_NOPE + A_VDIM)
    wk = jnp.pad(ukv[..., :A_NOPE], ((0, 0), (0, 0), (0, A_HEAD_PAD - A_NOPE)))
    wk = wk.reshape(A_KV_LORA, A_HEADS * A_HEAD_PAD).astype(BF16)
    wv = jnp.pad(ukv[..., A_NOPE:], ((0, 0), (0, 0), (0, A_HEAD_PAD - A_VDIM)))
    wv = wv.reshape(A_KV_LORA, A_HEADS * A_HEAD_PAD).astype(BF16)
    return wa, wq, wqr, wk, wv


def _rope_tables(positions):
    inv = ROPE_THETA ** (-jnp.arange(0, A_ROPE, 2, dtype=F32) / A_ROPE)
    ang = positions.astype(F32).reshape(-1, 1) * inv
    place = lambda t: jnp.pad(jnp.concatenate([t, t], axis=-1), ((0, 0), (A_NOPE, LANES - A_NOPE - A_ROPE)))
    return place(jnp.cos(ang)), place(jnp.sin(ang))


def _lower_bounds(logits):
    pr = jax.nn.softmax(logits.astype(F32), axis=0)
    return jnp.clip(jnp.cumsum(pr, axis=0) - pr[0:1], 0.0, 1.0 - 1e-6)


def _lane_row(v):
    return jnp.pad(v.astype(F32), (0, LANES - v.shape[0])).reshape(1, LANES)


def kernel(x, p, positions, w_in, a_q_norm, a_w_uq, a_kv_norm, a_w_ukv, b_conv, b_a_log, b_dt_bias, b_norm,
           c_lb_logits, c_norm, w_branch, w_out, ple_proj, ple_gate, ln_g, ln_b):
    batch, seq, _ = x.shape
    t = batch * seq
    cos_t, sin_t = _rope_tables(positions)
    lower_bounds = _lower_bounds(c_lb_logits)
    xf = x.reshape(t, D_MODEL)
    xb = xf.astype(BF16)
    for i in range(DEPTH):
        wi = w_in[i]
        wa, wq, wqr, wk, wv = _mla_weights(wi, a_w_uq[i], a_w_ukv[i])
        q, k, v = _mla_prep(xb, cos_t, sin_t, wa, a_q_norm[i].reshape(1, -1), a_kv_norm[i].reshape(1, -1),
                            wq, wqr, wk, wv)
        o_a = _mla_flash(q, k, v, batch, seq)

        gate_cols = jnp.concatenate([_cols(wi, 5), _cols(wi, 6)], axis=1)
        wb = jnp.concatenate([_cols(wi, 4), _pad_cols(gate_cols, 0, LANES)], axis=1).astype(BF16)
        o_b = _gdn(xb, wb, b_conv[i].reshape(B_CONV, -1), _lane_row(b_a_log[i]), _lane_row(b_dt_bias[i]),
                   b_norm[i].reshape(1, -1), batch, seq)

        wc = jnp.concatenate([_cols(wi, 8), _cols(wi, 9), _cols(wi, 10)], axis=1).astype(BF16)
        o_c = _hgrn(xb, wc, lower_bounds[i].reshape(1, -1), c_norm[i].reshape(1, -1), batch, seq)

        wz = jnp.concatenate([_cols(wi, 3), _cols(wi, 7), _cols(wi, 11)], axis=1).astype(BF16)
        xf, xb = _merge(xf, xb, o_a, o_b, o_c, p[i].reshape(t, PLE_DIM), wz, _cols(wi, 12).astype(BF16),
                        w_branch[i].astype(BF16), w_out[i].astype(BF16), ple_gate[i].astype(BF16),
                        ple_proj[i].astype(BF16), ln_g[i].reshape(1, -1), ln_b[i].reshape(1, -1))
    return xf.reshape(batch, seq, D_MODEL)
```

```python
import functools

import numpy as np
import jax
import jax.numpy as jnp
from jax import lax
from jax.experimental import pallas as pl
from jax.experimental.pallas import tpu as pltpu

F32 = jnp.float32
BF16 = jnp.bfloat16
HIGHEST = lax.Precision.HIGHEST

D_MODEL = 1024
DEPTH = 4
PLE_DIM = 256
BRANCH_W = 512
N_BRANCH = 3
A_HEADS = 8
A_NOPE = 64
A_ROPE = 32
A_VDIM = 64
A_Q_LORA = 256
A_KV_LORA = 128
A_SCALE = (A_NOPE + A_ROPE) ** -0.5
ROPE_THETA = 10000.0
B_HEADS = 4
B_DK = 128
B_DV = 128
B_CONV = 4
B_CHUNK = 64
C_HEADS = 4
C_DK = 128
C_DV = 128
NORM_EPS = 1e-6
MASK_VALUE = -1e30
DEEPNORM_ALPHA = (2.0 * DEPTH) ** 0.25
LOG2_E = 1.4426950408889634

IN_SPLITS = (A_Q_LORA, A_KV_LORA, A_ROPE, BRANCH_W, 3 * B_HEADS * B_DK, B_HEADS, B_HEADS, BRANCH_W,
             C_HEADS * C_DK, C_HEADS * C_DK, C_HEADS * C_DV, BRANCH_W, N_BRANCH * D_MODEL)
IN_OFFSETS = tuple(int(o) for o in np.cumsum((0,) + IN_SPLITS))

LANES = 128
SUBLANES = 8
A_HEAD_PAD = LANES
A_VT_ROWS = 80
VMEM_LIMIT = 56 * 1024 * 1024

TOKEN_TILE = 512
ATTN_TILE = 512
ATTN_GROUP = 4
REC_TILE = 256
HGRN_SUB = SUBLANES
HGRN_GROUP = 16


def _sigmoid(x):
    return 1.0 / (1.0 + jnp.exp(-x))


def _silu(x):
    return x * _sigmoid(x)


def _softplus(x):
    return jnp.maximum(x, 0.0) + jnp.log(1.0 + jnp.exp(-jnp.abs(x)))


def _rms(x, g):
    return x * lax.rsqrt(jnp.mean(x * x, axis=-1, keepdims=True) + NORM_EPS) * g


def _dot(a, b):
    return jnp.dot(a, b, preferred_element_type=F32)


def _dot_nt(a, b):
    return lax.dot_general(a, b, (((1,), (1,)), ((), ())), preferred_element_type=F32)


def _dot_tn(a, b):
    return lax.dot_general(a, b, (((0,), (0,)), ((), ())), preferred_element_type=F32)


def _dot_hi(a, b):
    return jnp.dot(a, b, preferred_element_type=F32, precision=HIGHEST)


def _const_spec(shape):
    nd = len(shape)
    return pl.BlockSpec(shape, lambda *_: (0,) * nd, pipeline_mode=pl.Buffered(1))


def _params(*sem):
    return pltpu.CompilerParams(dimension_semantics=sem, vmem_limit_bytes=VMEM_LIMIT)


def _mla_prep_kernel(xb_ref, cos_ref, sin_ref, wa_ref, qg_ref, kvg_ref, wq_ref, wqr_ref, wk_ref, wv_ref,
                     q_ref, k_ref, vt_ref):
    h = _dot(xb_ref[...], wa_ref[...])
    cq = h[:, :A_Q_LORA]
    ckv = h[:, A_Q_LORA:A_Q_LORA + A_KV_LORA]
    kr = h[:, A_Q_LORA + A_KV_LORA:A_Q_LORA + A_KV_LORA + LANES]
    kr_rot = h[:, A_Q_LORA + A_KV_LORA + LANES:]
    cqn = _rms(cq, qg_ref[...]).astype(BF16)
    ckvn = _rms(ckv, kvg_ref[...]).astype(BF16)
    cos = cos_ref[...]
    sin = sin_ref[...]
    lane = lax.broadcasted_iota(jnp.int32, cos.shape, 1)
    cq_tab = jnp.where(lane < A_NOPE, 1.0, cos) * (A_SCALE * LOG2_E)
    sq_tab = sin * (A_SCALE * LOG2_E)
    cq_tab = jnp.concatenate([cq_tab] * A_HEADS, axis=1)
    sq_tab = jnp.concatenate([sq_tab] * A_HEADS, axis=1)
    q = _dot(cqn, wq_ref[...]) * cq_tab + _dot(cqn, wqr_ref[...]) * sq_tab
    q_ref[...] = q.astype(BF16)
    k_rope = kr * cos + kr_rot * sin
    k = _dot(ckvn, wk_ref[...]) + jnp.concatenate([k_rope] * A_HEADS, axis=1)
    k_ref[...] = k.astype(BF16)
    vt = _dot_nt(wv_ref[...], ckvn)
    vrow = lax.broadcasted_iota(jnp.int32, vt.shape, 0) % A_VT_ROWS
    vt_ref[...] = jnp.where(vrow == A_VDIM, 1.0, vt).astype(BF16)


def _mla_prep(xb, cos_t, sin_t, wa, qg, kvg, wq, wqr, wk, wv):
    t = xb.shape[0]
    tm = min(TOKEN_TILE, t)
    row = lambda i: (i, 0)
    hp = A_HEADS * A_HEAD_PAD
    return pl.pallas_call(
        _mla_prep_kernel,
        grid=(t // tm,),
        in_specs=[pl.BlockSpec((tm, D_MODEL), row), pl.BlockSpec((tm, LANES), row), pl.BlockSpec((tm, LANES), row),
                  _const_spec(wa.shape), _const_spec(qg.shape), _const_spec(kvg.shape), _const_spec(wq.shape),
                  _const_spec(wqr.shape), _const_spec(wk.shape), _const_spec(wv.shape)],
        out_specs=[pl.BlockSpec((tm, hp), row), pl.BlockSpec((tm, hp), row),
                   pl.BlockSpec((A_HEADS * A_VT_ROWS, tm), lambda i: (0, i))],
        out_shape=[jax.ShapeDtypeStruct((t, hp), BF16), jax.ShapeDtypeStruct((t, hp), BF16),
                   jax.ShapeDtypeStruct((A_HEADS * A_VT_ROWS, t), BF16)],
        compiler_params=_params("parallel"),
        name="mla_prep",
    )(xb, cos_t, sin_t, wa, qg, kvg, wq, wqr, wk, wv)


def _flash_kernel(q_ref, k_ref, vt_ref, o_ref, *, blk):
    qi = pl.program_id(2)
    heads = tuple(slice(hh * A_HEAD_PAD, (hh + 1) * A_HEAD_PAD) for hh in range(2))
    qs = tuple(q_ref[:, hs] for hs in heads)
    key = lax.broadcasted_iota(jnp.int32, (blk, blk), 0)
    qry = lax.broadcasted_iota(jnp.int32, (blk, blk), 1)

    def scores(j):
        kk = k_ref[pl.ds(pl.multiple_of(j * blk, blk), blk), :]
        return [_dot_nt(kk[:, heads[hh]], qs[hh]) for hh in range(2)]

    def update(carry, j, s, masked):
        cols = pl.ds(pl.multiple_of(j * blk, blk), blk)
        out = []
        for hh in range(2):
            m, acc = carry[2 * hh:2 * hh + 2]
            sh = jnp.where(key <= qry, s[hh], MASK_VALUE) if masked else s[hh]
            m_new = jnp.maximum(m, jnp.max(sh, axis=0, keepdims=True))
            p = jnp.exp2(sh - m_new).astype(BF16)
            vt = vt_ref[hh * A_VT_ROWS:(hh + 1) * A_VT_ROWS, cols]
            acc = jnp.exp2(m - m_new) * acc + _dot(vt, p)
            out += [m_new, acc]
        return tuple(out)

    def group(i, carry):
        js = [ATTN_GROUP * i + t for t in range(ATTN_GROUP)]
        ss = [scores(js[0]), scores(js[1])]
        for t in range(ATTN_GROUP):
            if t + 2 < ATTN_GROUP:
                ss.append(scores(js[t + 2]))
            carry = update(carry, js[t], ss[t], False)
        return carry

    init = (jnp.full((1, blk), -jnp.inf, F32), jnp.zeros((A_VT_ROWS, blk), F32)) * 2
    groups = qi // ATTN_GROUP
    carry = lax.fori_loop(0, groups, group, init)
    carry = lax.fori_loop(groups * ATTN_GROUP, qi, lambda j, c: update(c, j, scores(j), False), carry)
    _, acc0, _, acc1 = update(carry, qi, scores(qi), True)
    o_t = jnp.concatenate([acc0[:A_VDIM] / acc0[A_VDIM:A_VDIM + 1], acc1[:A_VDIM] / acc1[A_VDIM:A_VDIM + 1]], axis=0)
    o_ref[...] = o_t.T


def _mla_flash(q, k, vt, batch, seq):
    blk = min(ATTN_TILE, seq)
    nq = seq // blk
    return pl.pallas_call(
        functools.partial(_flash_kernel, blk=blk),
        grid=(batch, A_HEADS // 2, nq),
        in_specs=[pl.BlockSpec((blk, 2 * A_HEAD_PAD), lambda b, h, i: (b * nq + i, h)),
                  pl.BlockSpec((seq, 2 * A_HEAD_PAD), lambda b, h, i: (b, h)),
                  pl.BlockSpec((2 * A_VT_ROWS, seq), lambda b, h, i: (h, b))],
        out_specs=pl.BlockSpec((blk, 2 * A_VDIM), lambda b, h, i: (b * nq + i, h)),
        out_shape=jax.ShapeDtypeStruct((batch * seq, BRANCH_W), F32),
        compiler_params=_params("parallel", "parallel", "arbitrary"),
        name="mla_flash",
    )(q, k, vt)


def _cumsum_rows(x, period):
    n = x.shape[0]
    pos = lax.broadcasted_iota(jnp.int32, x.shape, 0) % period
    s = 1
    while s < period:
        x = x + jnp.where(pos >= s, pltpu.roll(x, s, 0), 0.0)
        s *= 2
    return x


def _bdot(a, b):
    return lax.dot_general(a, b, (((2,), (1,)), ((0,), (0,))), preferred_element_type=F32)


def _bdot_nt(a, b):
    return lax.dot_general(a, b, (((2,), (2,)), ((0,), (0,))), preferred_element_type=F32)


def _unit_lower_inverse(low):
    c = low.shape[-1]
    eye = (lax.broadcasted_iota(jnp.int32, (c, c), 0) == lax.broadcasted_iota(jnp.int32, (c, c), 1)).astype(F32)
    inv = eye - low
    pw16 = low.astype(BF16)
    k = 2
    while k < c:
        pw16 = _bdot(pw16, pw16).astype(BF16)
        inv = inv + _bdot(inv.astype(BF16), pw16)
        k *= 2
    return inv


def _gdn_kernel(xb_ref, w_ref, convw_ref, alog_ref, dtb_ref, ng_ref, o_ref,
                state_ref, xbuf_ref, m_s, n_s, qp_s, o0_s, egl_s, out_s, *, tile):
    c = B_CHUNK
    n = tile // c
    nb = B_HEADS * n
    hd = B_HEADS * B_DK

    @pl.when(pl.program_id(1) == 0)
    def _():
        state_ref[...] = jnp.zeros_like(state_ref)
        xbuf_ref[0:SUBLANES, :] = jnp.zeros((SUBLANES, 3 * hd), F32)

    h = _dot(xb_ref[...], w_ref[...])
    xbuf_ref[SUBLANES:, :] = h[:, :3 * hd]
    conv = h[:, :3 * hd] * convw_ref[B_CONV - 1:B_CONV, :]
    for s in range(1, B_CONV):
        conv = conv + xbuf_ref[pl.ds(SUBLANES - s, tile), :] * convw_ref[B_CONV - 1 - s:B_CONV - s, :]
    xbuf_ref[0:SUBLANES, :] = h[tile - SUBLANES:, :3 * hd]
    qkv = _silu(conv)

    gates = h[:, 3 * hd:]
    g_all = -jnp.exp(alog_ref[...]) * _softplus(gates + dtb_ref[...])
    gcum = _cumsum_rows(g_all, c)
    beta_all = _sigmoid(gates)

    def per_head(fn):
        return jnp.concatenate([fn(hh) for hh in range(B_HEADS)], axis=0)

    def unit(x):
        return x * lax.rsqrt(jnp.sum(x * x, axis=-1, keepdims=True) + NORM_EPS)

    q = per_head(lambda hh: (unit(qkv[:, hh * B_DK:(hh + 1) * B_DK]) * (B_DK ** -0.5)).reshape(n, c, B_DK))
    k = per_head(lambda hh: unit(qkv[:, hd + hh * B_DK:hd + (hh + 1) * B_DK]).reshape(n, c, B_DK))
    v = per_head(lambda hh: qkv[:, 2 * hd + hh * B_DV:2 * hd + (hh + 1) * B_DV].reshape(n, c, B_DV))
    gc = per_head(lambda hh: gcum[:, hh:hh + 1].reshape(n, c, 1))
    beta = per_head(lambda hh: beta_all[:, B_HEADS + hh:B_HEADS + hh + 1].reshape(n, c, 1))

    ri = lax.broadcasted_iota(jnp.int32, (c, c), 0)
    ci = lax.broadcasted_iota(jnp.int32, (c, c), 1)
    tril = ci <= ri
    gcb = jnp.broadcast_to(gc, (nb, c, c))
    diff = gcb - jnp.swapaxes(gcb, 1, 2)
    decay = jnp.where(tril, jnp.exp(jnp.where(tril, diff, 0.0)), 0.0)
    k16 = k.astype(BF16)
    k_beta = k * beta
    low = jnp.where(ci < ri, _bdot_nt(k_beta.astype(BF16), k16) * decay, 0.0)
    inv = _unit_lower_inverse(low)
    eg = jnp.exp(gc)
    rhs = jnp.concatenate([v * beta, k_beta * eg], axis=2)
    uw16 = _bdot(inv.astype(BF16), rhs.astype(BF16)).astype(BF16)
    intra16 = (_bdot_nt(q.astype(BF16), k16) * decay).astype(BF16)
    g_last = gc[:, c - 1:c, :]
    kd_t = jnp.swapaxes(k * jnp.exp(g_last - gc), 1, 2).astype(BF16)
    kuw = _bdot(kd_t, uw16)
    iuw = _bdot(intra16, uw16)
    n_s[...] = kuw[:, :, :B_DV]
    m_s[...] = kuw[:, :, B_DV:].astype(BF16)
    o0_s[...] = iuw[:, :, :B_DV]
    qp_s[...] = (q * eg - iuw[:, :, B_DV:]).astype(BF16)
    egl_s[...] = jnp.broadcast_to(jnp.exp(g_last), (nb, 1, B_DV))

    heads = range(B_HEADS)
    st = [state_ref[hh] for hh in heads]
    for nn in range(n):
        st16 = [s.astype(BF16) for s in st]
        ms = [_dot(m_s[hh * n + nn], st16[hh]) for hh in heads]
        for hh in heads:
            out_s[nn * c:(nn + 1) * c, hh * B_DV:(hh + 1) * B_DV] = _dot(qp_s[hh * n + nn], st16[hh]) + o0_s[hh * n + nn]
        st = [st[hh] * egl_s[hh * n + nn] - ms[hh] + n_s[hh * n + nn] for hh in heads]
    for hh in heads:
        state_ref[hh] = st[hh]
        o_ref[:, hh * B_DV:(hh + 1) * B_DV] = _rms(out_s[:, hh * B_DV:(hh + 1) * B_DV], ng_ref[...])


def _gdn(xb, w, conv_w, a_log, dt_bias, norm_g, batch, seq):
    tile = min(REC_TILE, seq)
    nt = seq // tile
    hd = B_HEADS * B_DK
    nb = B_HEADS * (tile // B_CHUNK)
    return pl.pallas_call(
        functools.partial(_gdn_kernel, tile=tile),
        grid=(batch, nt),
        in_specs=[pl.BlockSpec((tile, D_MODEL), lambda b, i: (b * nt + i, 0)),
                  _const_spec(w.shape), _const_spec(conv_w.shape), _const_spec(a_log.shape),
                  _const_spec(dt_bias.shape), _const_spec(norm_g.shape)],
        out_specs=pl.BlockSpec((tile, BRANCH_W), lambda b, i: (b * nt + i, 0)),
        out_shape=jax.ShapeDtypeStruct((batch * seq, BRANCH_W), F32),
        scratch_shapes=[pltpu.VMEM((B_HEADS, B_DK, B_DV), F32),
                        pltpu.VMEM((SUBLANES + tile, 3 * hd), F32),
                        pltpu.VMEM((nb, B_DK, B_DK), BF16),
                        pltpu.VMEM((nb, B_DK, B_DV), F32),
                        pltpu.VMEM((nb, B_CHUNK, B_DK), BF16),
                        pltpu.VMEM((nb, B_CHUNK, B_DV), F32),
                        pltpu.VMEM((nb, 1, B_DV), F32),
                        pltpu.VMEM((tile, hd), F32)],
        compiler_params=_params("arbitrary", "arbitrary"),
        name="gdn",
    )(xb, w, conv_w, a_log, dt_bias, norm_g)


def _hgrn_levels(tile):
    out = []
    lv = tile // 2
    while lv >= HGRN_SUB:
        out.append(lv)
        lv //= 2
    return tuple(out)


def _hgrn_level_map(tile):
    i = np.arange(tile)[:, None]
    j = np.arange(tile)[None, :]
    x = np.bitwise_xor(i, j)
    lev = np.where(j < i, np.floor(np.log2(np.maximum(x, 1))).astype(np.int32), -1)
    return jnp.asarray(lev, jnp.int32)


def _hgrn_kernel(xb_ref, w_ref, lb_ref, ng_ref, lev_ref, tri_ref, o_ref,
                 state_ref, q_s, k_s, v_s, cum_s, acc_s, *, tile):
    hd = C_HEADS * C_DK

    @pl.when(pl.program_id(1) == 0)
    def _():
        state_ref[...] = jnp.zeros_like(state_ref)

    h = _dot(xb_ref[...], w_ref[...])
    lev = lev_ref[...]
    sub_row = lax.broadcasted_iota(jnp.int32, (HGRN_SUB, C_DV), 0)

    lb = lb_ref[...]
    q_all = _silu(h[:, :hd]) * (C_DK ** -0.5)
    sig = _sigmoid(h[:, hd:2 * hd])
    log_f = jnp.log(jnp.maximum(lb + (1.0 - lb) * sig, 1e-30))
    k_all = (1.0 - lb) * (1.0 - sig)
    f_hi = log_f.astype(BF16)
    r1 = log_f - f_hi.astype(F32)
    f_mid = r1.astype(BF16)
    f_lo = (r1 - f_mid.astype(F32)).astype(BF16)
    parts = _dot(tri_ref[...], jnp.concatenate([f_hi, f_mid, f_lo], axis=1))
    cum_all = parts[:, :hd] + parts[:, hd:2 * hd] + parts[:, 2 * hd:]

    for hh in range(C_HEADS):
        sl = slice(hh * C_DK, (hh + 1) * C_DK)
        q = q_all[:, sl]
        k = k_all[:, sl]
        v = h[:, 2 * hd + hh * C_DV:2 * hd + (hh + 1) * C_DV]
        cum = cum_all[:, sl]

        state_t = state_ref[hh]
        out = _dot_nt((q * jnp.exp(cum)).astype(BF16), state_t.astype(BF16))
        cum_last = cum[tile - 1:tile, :]
        k_dec = k * jnp.exp(cum_last - cum)
        v16 = v.astype(BF16)
        state_ref[hh] = state_t * jnp.exp(cum_last) + _dot_tn(v16, k_dec.astype(BF16))

        scores = jnp.zeros((tile, tile), F32)
        for lv in _hgrn_levels(tile):
            blocks = tile // (2 * lv)
            cum3 = cum.reshape(blocks, 2 * lv, C_DK)
            ref = jnp.broadcast_to(cum3[:, lv:lv + 1, :], cum3.shape).reshape(tile, C_DK)
            qt = q * jnp.exp(jnp.minimum(cum - ref, 0.0))
            kt = k * jnp.exp(jnp.minimum(ref - cum, 0.0))
            s_lv = _dot_nt(qt.astype(BF16), kt.astype(BF16))
            scores = jnp.where(lev == int(np.log2(lv)), s_lv, scores)
        acc_s[...] = out + _dot(scores.astype(BF16), v16)

        q_s[...] = q
        k_s[...] = k
        v_s[...] = v
        cum_s[...] = cum

        def group(gi, _):
            base = pl.multiple_of(gi * (HGRN_GROUP * HGRN_SUB), HGRN_GROUP * HGRN_SUB)
            blocks = [pl.ds(base + ss * HGRN_SUB, HGRN_SUB) for ss in range(HGRN_GROUP)]
            qv = [q_s[r, :] for r in blocks]
            cv = [cum_s[r, :] for r in blocks]
            acc = [acc_s[r, :] for r in blocks]
            for j in range(HGRN_SUB):
                for ss in range(HGRN_GROUP):
                    rj = pl.ds(base + ss * HGRN_SUB + j, 1)
                    e = jnp.exp(jnp.minimum(cv[ss] - cum_s[rj, :], 0.0))
                    sc = jnp.sum(qv[ss] * k_s[rj, :] * e, axis=1, keepdims=True)
                    acc[ss] = acc[ss] + jnp.where(sub_row >= j, sc, 0.0) * v_s[rj, :]
            for ss in range(HGRN_GROUP):
                acc_s[blocks[ss], :] = acc[ss]
            return 0

        lax.fori_loop(0, tile // (HGRN_GROUP * HGRN_SUB), group, 0)
        o_ref[:, sl] = _rms(acc_s[...], ng_ref[...])


def _hgrn(xb, w, lower_bound, norm_g, batch, seq):
    tile = min(REC_TILE, seq)
    nt = seq // tile
    lev = _hgrn_level_map(tile)
    tri = jnp.asarray(np.tril(np.ones((tile, tile), np.float32)), BF16)
    return pl.pallas_call(
        functools.partial(_hgrn_kernel, tile=tile),
        grid=(batch, nt),
        in_specs=[pl.BlockSpec((tile, D_MODEL), lambda b, i: (b * nt + i, 0)),
                  _const_spec(w.shape), _const_spec(lower_bound.shape), _const_spec(norm_g.shape),
                  _const_spec(lev.shape), _const_spec(tri.shape)],
        out_specs=pl.BlockSpec((tile, BRANCH_W), lambda b, i: (b * nt + i, 0)),
        out_shape=jax.ShapeDtypeStruct((batch * seq, BRANCH_W), F32),
        scratch_shapes=[pltpu.VMEM((C_HEADS, C_DV, C_DK), F32)]
                       + [pltpu.VMEM((tile, C_DK), F32) for _ in range(5)],
        compiler_params=_params("arbitrary", "arbitrary"),
        name="hgrn2",
    )(xb, w, lower_bound, norm_g, lev, tri)


def _merge_kernel(x_ref, xb_ref, oa_ref, ob_ref, oc_ref, p_ref, wz_ref, wg_ref, wbr_ref, wout_ref,
                  pgate_ref, pproj_ref, lng_ref, lnb_ref, xo_ref, xbo_ref):
    xb = xb_ref[...]
    z = _dot(xb, wz_ref[...])
    merged = None
    for n, o_ref in enumerate((oa_ref, ob_ref, oc_ref)):
        y = o_ref[...] * _silu(z[:, n * BRANCH_W:(n + 1) * BRANCH_W])
        proj = _dot(y.astype(BF16), wbr_ref[n])
        gate = _sigmoid(_dot(xb, wg_ref[:, n * D_MODEL:(n + 1) * D_MODEL]))
        merged = gate * proj if merged is None else merged + gate * proj
    r = DEEPNORM_ALPHA * x_ref[...] + _dot(merged.astype(BF16), wout_ref[...])
    ple = _dot(p_ref[...].astype(BF16), pproj_ref[...])
    r = r + _sigmoid(_dot(r.astype(BF16), pgate_ref[...])) * ple
    mu = jnp.mean(r, axis=-1, keepdims=True)
    var = jnp.mean(jnp.square(r - mu), axis=-1, keepdims=True)
    xn = (r - mu) * lax.rsqrt(var + NORM_EPS) * lng_ref[...] + lnb_ref[...]
    xo_ref[...] = xn
    xbo_ref[...] = xn.astype(BF16)


def _merge(x, xb, oa, ob, oc, p, wz, wg, wbr, wout, pgate, pproj, lng, lnb):
    t = x.shape[0]
    tm = min(TOKEN_TILE, t)
    row = lambda i: (i, 0)
    return pl.pallas_call(
        _merge_kernel,
        grid=(t // tm,),
        in_specs=[pl.BlockSpec((tm, D_MODEL), row), pl.BlockSpec((tm, D_MODEL), row),
                  pl.BlockSpec((tm, BRANCH_W), row), pl.BlockSpec((tm, BRANCH_W), row),
                  pl.BlockSpec((tm, BRANCH_W), row), pl.BlockSpec((tm, PLE_DIM), row),
                  _const_spec(wz.shape), _const_spec(wg.shape), _const_spec(wbr.shape), _const_spec(wout.shape),
                  _const_spec(pgate.shape), _const_spec(pproj.shape), _const_spec(lng.shape),
                  _const_spec(lnb.shape)],
        out_specs=[pl.BlockSpec((tm, D_MODEL), row), pl.BlockSpec((tm, D_MODEL), row)],
        out_shape=[jax.ShapeDtypeStruct((t, D_MODEL), F32), jax.ShapeDtypeStruct((t, D_MODEL), BF16)],
        compiler_params=_params("parallel"),
        name="merge",
    )(x, xb, oa, ob, oc, p, wz, wg, wbr, wout, pgate, pproj, lng, lnb)


def _cols(w, idx):
    return w[:, IN_OFFSETS[idx]:IN_OFFSETS[idx + 1]]


def _rot_cols(w):
    half = w.shape[-1] // 2
    return jnp.concatenate([-w[..., half:], w[..., :half]], axis=-1)


def _pad_cols(w, left, total):
    return jnp.pad(w, ((0, 0), (left, total - left - w.shape[-1])))


def _mla_weights(w_in, w_uq, w_ukv):
    w_kr = _cols(w_in, 2)
    wa = jnp.concatenate([_cols(w_in, 0), _cols(w_in, 1), _pad_cols(w_kr, A_NOPE, LANES),
                          _pad_cols(_rot_cols(w_kr), A_NOPE, LANES)], axis=1).astype(BF16)
    uq = w_uq.reshape(A_Q_LORA, A_HEADS, A_NOPE + A_ROPE)
    pad = A_HEAD_PAD - A_NOPE - A_ROPE
    wq = jnp.pad(uq, ((0, 0), (0, 0), (0, pad))).reshape(A_Q_LORA, A_HEADS * A_HEAD_PAD).astype(BF16)
    uq_rot = jnp.concatenate([jnp.zeros_like(uq[..., :A_NOPE]), _rot_cols(uq[..., A_NOPE:])], axis=-1)
    wqr = jnp.pad(uq_rot, ((0, 0), (0, 0), (0, pad))).reshape(A_Q_LORA, A_HEADS * A_HEAD_PAD).astype(BF16)
    ukv = w_ukv.reshape(A_KV_LORA, A_HEADS, A_NOPE + A_VDIM)
    wk = jnp.pad(ukv[..., :A_NOPE], ((0, 0), (0, 0), (0, A_HEAD_PAD - A_NOPE)))
    wk = wk.reshape(A_KV_LORA, A_HEADS * A_HEAD_PAD).astype(BF16)
    wv = jnp.pad(jnp.transpose(ukv[..., A_NOPE:], (1, 2, 0)), ((0, 0), (0, A_VT_ROWS - A_VDIM), (0, 0)))
    wv = wv.reshape(A_HEADS * A_VT_ROWS, A_KV_LORA).astype(BF16)
    return wa, wq, wqr, wk, wv


def _rope_tables(positions):
    inv = ROPE_THETA ** (-jnp.arange(0, A_ROPE, 2, dtype=F32) / A_ROPE)
    ang = positions.astype(F32).reshape(-1, 1) * inv
    place = lambda t: jnp.pad(jnp.concatenate([t, t], axis=-1), ((0, 0), (A_NOPE, LANES - A_NOPE - A_ROPE)))
    return place(jnp.cos(ang)), place(jnp.sin(ang))


def _lower_bounds(logits):
    pr = jax.nn.softmax(logits.astype(F32), axis=0)
    return jnp.clip(jnp.cumsum(pr, axis=0) - pr[0:1], 0.0, 1.0 - 1e-6)


def _lane_row(v):
    return jnp.pad(v.astype(F32), (0, LANES - v.shape[0])).reshape(1, LANES)


def kernel(x, p, positions, w_in, a_q_norm, a_w_uq, a_kv_norm, a_w_ukv, b_conv, b_a_log, b_dt_bias, b_norm,
           c_lb_logits, c_norm, w_branch, w_out, ple_proj, ple_gate, ln_g, ln_b):
    batch, seq, _ = x.shape
    t = batch * seq
    cos_t, sin_t = _rope_tables(positions)
    lower_bounds = _lower_bounds(c_lb_logits)
    xf = x.reshape(t, D_MODEL)
    xb = xf.astype(BF16)
    for i in range(DEPTH):
        wi = w_in[i]
        wa, wq, wqr, wk, wv = _mla_weights(wi, a_w_uq[i], a_w_ukv[i])
        q, k, v = _mla_prep(xb, cos_t, sin_t, wa, a_q_norm[i].reshape(1, -1), a_kv_norm[i].reshape(1, -1),
                            wq, wqr, wk, wv)
        o_a = _mla_flash(q, k, v, batch, seq)

        gate_cols = jnp.concatenate([_cols(wi, 5), _cols(wi, 6)], axis=1)
        wb = jnp.concatenate([_cols(wi, 4), _pad_cols(gate_cols, 0, LANES)], axis=1).astype(BF16)
        o_b = _gdn(xb, wb, b_conv[i].reshape(B_CONV, -1), _lane_row(b_a_log[i]), _lane_row(b_dt_bias[i]),
                   b_norm[i].reshape(1, -1), batch, seq)

        wc = jnp.concatenate([_cols(wi, 8), _cols(wi, 9), _cols(wi, 10)], axis=1).astype(BF16)
        o_c = _hgrn(xb, wc, lower_bounds[i].reshape(1, -1), c_norm[i].reshape(1, -1), batch, seq)

        wz = jnp.concatenate([_cols(wi, 3), _cols(wi, 7), _cols(wi, 11)], axis=1).astype(BF16)
        xf, xb = _merge(xf, xb, o_a, o_b, o_c, p[i].reshape(t, PLE_DIM), wz, _cols(wi, 12).astype(BF16),
                        w_branch[i].astype(BF16), w_out[i].astype(BF16), ple_gate[i].astype(BF16),
                        ple_proj[i].astype(BF16), ln_g[i].reshape(1, -1), ln_b[i].reshape(1, -1))
    return xf.reshape(batch, seq, D_MODEL)
```

```python
import functools

import numpy as np
import jax
import jax.numpy as jnp
from jax import lax
from jax.experimental import pallas as pl
from jax.experimental.pallas import tpu as pltpu

F32 = jnp.float32
BF16 = jnp.bfloat16
HIGHEST = lax.Precision.HIGHEST

D_MODEL = 1024
DEPTH = 4
PLE_DIM = 256
BRANCH_W = 512
N_BRANCH = 3
A_HEADS = 8
A_NOPE = 64
A_ROPE = 32
A_VDIM = 64
A_Q_LORA = 256
A_KV_LORA = 128
A_SCALE = (A_NOPE + A_ROPE) ** -0.5
ROPE_THETA = 10000.0
B_HEADS = 4
B_DK = 128
B_DV = 128
B_CONV = 4
B_CHUNK = 64
C_HEADS = 4
C_DK = 128
C_DV = 128
NORM_EPS = 1e-6
MASK_VALUE = -1e30
DEEPNORM_ALPHA = (2.0 * DEPTH) ** 0.25
LOG2_E = 1.4426950408889634

IN_SPLITS = (A_Q_LORA, A_KV_LORA, A_ROPE, BRANCH_W, 3 * B_HEADS * B_DK, B_HEADS, B_HEADS, BRANCH_W,
             C_HEADS * C_DK, C_HEADS * C_DK, C_HEADS * C_DV, BRANCH_W, N_BRANCH * D_MODEL)
IN_OFFSETS = tuple(int(o) for o in np.cumsum((0,) + IN_SPLITS))

LANES = 128
SUBLANES = 8
A_HEAD_PAD = LANES
VMEM_LIMIT = 56 * 1024 * 1024

TOKEN_TILE = 512
ATTN_TILE = 512
ATTN_GROUP = 4
REC_TILE = 256
GDN_TILE = 512
HGRN_SUB = SUBLANES
HGRN_GROUP = 16


def _sigmoid(x):
    return 1.0 / (1.0 + jnp.exp(-x))


def _silu(x):
    return x * _sigmoid(x)


def _softplus(x):
    return jnp.maximum(x, 0.0) + jnp.log(1.0 + jnp.exp(-jnp.abs(x)))


def _rms(x, g):
    return x * lax.rsqrt(jnp.mean(x * x, axis=-1, keepdims=True) + NORM_EPS) * g


def _dot(a, b):
    return jnp.dot(a, b, preferred_element_type=F32)


def _dot_nt(a, b):
    return lax.dot_general(a, b, (((1,), (1,)), ((), ())), preferred_element_type=F32)


def _dot_tn(a, b):
    return lax.dot_general(a, b, (((0,), (0,)), ((), ())), preferred_element_type=F32)


def _dot_hi(a, b):
    return jnp.dot(a, b, preferred_element_type=F32, precision=HIGHEST)


def _const_spec(shape):
    nd = len(shape)
    return pl.BlockSpec(shape, lambda *_: (0,) * nd, pipeline_mode=pl.Buffered(1))


def _layer_spec(arr, layer):
    nd = arr.ndim - 1
    return pl.BlockSpec((None,) + arr.shape[1:], lambda *_: (layer,) + (0,) * nd, pipeline_mode=pl.Buffered(1))


def _params(*sem):
    return pltpu.CompilerParams(dimension_semantics=sem, vmem_limit_bytes=VMEM_LIMIT)


def _mla_prep_kernel(xb_ref, cos_ref, sin_ref, wa_ref, qg_ref, kvg_ref, wq_ref, wqr_ref, wk_ref, wv_ref,
                     q_ref, k_ref, v_ref):
    h = _dot(xb_ref[...], wa_ref[...])
    cq = h[:, :A_Q_LORA]
    ckv = h[:, A_Q_LORA:A_Q_LORA + A_KV_LORA]
    kr = h[:, A_Q_LORA + A_KV_LORA:A_Q_LORA + A_KV_LORA + LANES]
    kr_rot = h[:, A_Q_LORA + A_KV_LORA + LANES:]
    cqn = _rms(cq, qg_ref[...]).astype(BF16)
    ckvn = _rms(ckv, kvg_ref[...]).astype(BF16)
    cos = cos_ref[...]
    sin = sin_ref[...]
    lane = lax.broadcasted_iota(jnp.int32, cos.shape, 1)
    cq_tab = jnp.where(lane < A_NOPE, 1.0, cos) * (A_SCALE * LOG2_E)
    sq_tab = sin * (A_SCALE * LOG2_E)
    cq_tab = jnp.concatenate([cq_tab] * A_HEADS, axis=1)
    sq_tab = jnp.concatenate([sq_tab] * A_HEADS, axis=1)
    q = _dot(cqn, wq_ref[...]) * cq_tab + _dot(cqn, wqr_ref[...]) * sq_tab
    q_ref[...] = q.astype(BF16)
    k_rope = kr * cos + kr_rot * sin
    k = _dot(ckvn, wk_ref[...]) + jnp.concatenate([k_rope] * A_HEADS, axis=1)
    k_ref[...] = k.astype(BF16)
    v = _dot(ckvn, wv_ref[...])
    vlane = lax.broadcasted_iota(jnp.int32, v.shape, 1) % A_HEAD_PAD
    v_ref[...] = jnp.where(vlane == A_VDIM, 1.0, v).astype(BF16)


def _mla_prep(layer, xb, cos_t, sin_t, wa, qg, kvg, wq, wqr, wk, wv):
    t = xb.shape[0]
    tm = min(TOKEN_TILE, t)
    row = lambda i: (i, 0)
    hp = A_HEADS * A_HEAD_PAD
    return pl.pallas_call(
        _mla_prep_kernel,
        grid=(t // tm,),
        in_specs=[pl.BlockSpec((tm, D_MODEL), row), pl.BlockSpec((tm, LANES), row), pl.BlockSpec((tm, LANES), row),
                  *[_layer_spec(a, layer) for a in (wa, qg, kvg, wq, wqr, wk, wv)]],
        out_specs=[pl.BlockSpec((tm, hp), row), pl.BlockSpec((tm, hp), row), pl.BlockSpec((tm, hp), row)],
        out_shape=[jax.ShapeDtypeStruct((t, hp), BF16)] * 3,
        compiler_params=_params("parallel"),
        name="mla_prep",
    )(xb, cos_t, sin_t, wa, qg, kvg, wq, wqr, wk, wv)


def _flash_kernel(q_ref, k_ref, v_ref, o_ref, *, blk):
    qi = pl.program_id(2)
    qs = (q_ref[:, :A_HEAD_PAD], q_ref[:, A_HEAD_PAD:])
    heads = tuple(slice(hh * A_HEAD_PAD, (hh + 1) * A_HEAD_PAD) for hh in range(2))
    row = lax.broadcasted_iota(jnp.int32, (blk, blk), 0)
    col = lax.broadcasted_iota(jnp.int32, (blk, blk), 1)

    def scores(j):
        kk = k_ref[pl.ds(pl.multiple_of(j * blk, blk), blk), :]
        return [_dot_nt(qs[hh], kk[:, heads[hh]]) for hh in range(2)]

    def update(carry, j, s, masked):
        vv = v_ref[pl.ds(pl.multiple_of(j * blk, blk), blk), :]
        out = []
        for hh in range(2):
            m, acc = carry[2 * hh:2 * hh + 2]
            sh = jnp.where(col <= row, s[hh], MASK_VALUE) if masked else s[hh]
            m_new = jnp.maximum(m, jnp.max(sh, axis=1, keepdims=True))
            p = jnp.exp2((sh - m_new).astype(BF16))
            acc = jnp.exp2(m - m_new) * acc + _dot(p, vv[:, heads[hh]])
            out += [m_new, acc]
        return tuple(out)

    def group(i, carry):
        js = [ATTN_GROUP * i + t for t in range(ATTN_GROUP)]
        ss = [scores(j) for j in js]
        for j, s in zip(js, ss):
            carry = update(carry, j, s, False)
        return carry

    init = (jnp.full((blk, 1), -jnp.inf, F32), jnp.zeros((blk, A_HEAD_PAD), F32)) * 2
    groups = qi // ATTN_GROUP
    carry = lax.fori_loop(0, groups, group, init)
    carry = lax.fori_loop(groups * ATTN_GROUP, qi, lambda j, c: update(c, j, scores(j), False), carry)
    _, acc0, _, acc1 = update(carry, qi, scores(qi), True)
    o0 = acc0 / acc0[:, A_VDIM:A_VDIM + 1]
    o1 = acc1 / acc1[:, A_VDIM:A_VDIM + 1]
    lane = lax.broadcasted_iota(jnp.int32, o0.shape, 1)
    o_ref[...] = jnp.where(lane < A_VDIM, o0, pltpu.roll(o1, A_VDIM, 1))


def _mla_flash(q, k, v, batch, seq):
    blk = min(ATTN_TILE, seq)
    nq = seq // blk
    return pl.pallas_call(
        functools.partial(_flash_kernel, blk=blk),
        grid=(batch, A_HEADS // 2, nq),
        in_specs=[pl.BlockSpec((blk, 2 * A_HEAD_PAD), lambda b, h, i: (b * nq + i, h)),
                  pl.BlockSpec((seq, 2 * A_HEAD_PAD), lambda b, h, i: (b, h)),
                  pl.BlockSpec((seq, 2 * A_HEAD_PAD), lambda b, h, i: (b, h))],
        out_specs=pl.BlockSpec((blk, 2 * A_VDIM), lambda b, h, i: (b * nq + i, h)),
        out_shape=jax.ShapeDtypeStruct((batch * seq, BRANCH_W), F32),
        compiler_params=_params("parallel", "parallel", "arbitrary"),
        name="mla_flash",
    )(q, k, v)


def _cumsum_rows(x, period):
    n = x.shape[0]
    pos = lax.broadcasted_iota(jnp.int32, x.shape, 0) % period
    s = 1
    while s < period:
        x = x + jnp.where(pos >= s, pltpu.roll(x, s, 0), 0.0)
        s *= 2
    return x


def _bdot(a, b):
    return lax.dot_general(a, b, (((2,), (1,)), ((0,), (0,))), preferred_element_type=F32)


def _bdot_nt(a, b):
    return lax.dot_general(a, b, (((2,), (2,)), ((0,), (0,))), preferred_element_type=F32)


def _unit_lower_inverse(low):
    c = low.shape[-1]
    eye = (lax.broadcasted_iota(jnp.int32, (c, c), 0) == lax.broadcasted_iota(jnp.int32, (c, c), 1)).astype(F32)
    inv = eye - low
    pw16 = low.astype(BF16)
    k = 2
    while k < c:
        pw16 = _bdot(pw16, pw16).astype(BF16)
        inv = inv + _bdot(inv.astype(BF16), pw16)
        k *= 2
    return inv


def _gdn_kernel(xb_ref, w_ref, convw_ref, alog_ref, dtb_ref, ng_ref, o_ref,
                state_ref, xbuf_ref, m_s, n_s, qp_s, o0_s, egl_s, out_s, *, tile):
    c = B_CHUNK
    n = tile // c
    nb = B_HEADS * n
    hd = B_HEADS * B_DK

    @pl.when(pl.program_id(1) == 0)
    def _():
        state_ref[...] = jnp.zeros_like(state_ref)
        xbuf_ref[0:SUBLANES, :] = jnp.zeros((SUBLANES, 3 * hd), F32)

    h = _dot(xb_ref[...], w_ref[...])
    xbuf_ref[SUBLANES:, :] = h[:, :3 * hd]
    conv = h[:, :3 * hd] * convw_ref[B_CONV - 1:B_CONV, :]
    for s in range(1, B_CONV):
        conv = conv + xbuf_ref[pl.ds(SUBLANES - s, tile), :] * convw_ref[B_CONV - 1 - s:B_CONV - s, :]
    xbuf_ref[0:SUBLANES, :] = h[tile - SUBLANES:, :3 * hd]
    qkv = _silu(conv)

    gates = h[:, 3 * hd:]
    g_all = -jnp.exp(alog_ref[...]) * _softplus(gates + dtb_ref[...])
    gcum = _cumsum_rows(g_all, c)
    beta_all = _sigmoid(gates)

    def per_head(fn):
        return jnp.concatenate([fn(hh) for hh in range(B_HEADS)], axis=0)

    def unit(x):
        return x * lax.rsqrt(jnp.sum(x * x, axis=-1, keepdims=True) + NORM_EPS)

    q = per_head(lambda hh: (unit(qkv[:, hh * B_DK:(hh + 1) * B_DK]) * (B_DK ** -0.5)).reshape(n, c, B_DK))
    k = per_head(lambda hh: unit(qkv[:, hd + hh * B_DK:hd + (hh + 1) * B_DK]).reshape(n, c, B_DK))
    v = per_head(lambda hh: qkv[:, 2 * hd + hh * B_DV:2 * hd + (hh + 1) * B_DV].reshape(n, c, B_DV))
    gc = per_head(lambda hh: gcum[:, hh:hh + 1].reshape(n, c, 1))
    beta = per_head(lambda hh: beta_all[:, B_HEADS + hh:B_HEADS + hh + 1].reshape(n, c, 1))

    ri = lax.broadcasted_iota(jnp.int32, (c, c), 0)
    ci = lax.broadcasted_iota(jnp.int32, (c, c), 1)
    tril = ci <= ri
    gcb = jnp.broadcast_to(gc, (nb, c, c))
    diff = gcb - jnp.swapaxes(gcb, 1, 2)
    decay = jnp.where(tril, jnp.exp(jnp.where(tril, diff, 0.0)), 0.0)
    k16 = k.astype(BF16)
    k_beta = k * beta
    low = jnp.where(ci < ri, _bdot_nt(k_beta.astype(BF16), k16) * decay, 0.0)
    inv = _unit_lower_inverse(low)
    eg = jnp.exp(gc)
    rhs = jnp.concatenate([v * beta, k_beta * eg], axis=2)
    uw16 = _bdot(inv.astype(BF16), rhs.astype(BF16)).astype(BF16)
    intra16 = (_bdot_nt(q.astype(BF16), k16) * decay).astype(BF16)
    g_last = gc[:, c - 1:c, :]
    kd_t = jnp.swapaxes(k * jnp.exp(g_last - gc), 1, 2).astype(BF16)
    kuw = _bdot(kd_t, uw16)
    iuw = _bdot(intra16, uw16)
    n_s[...] = kuw[:, :, :B_DV]
    m_s[...] = kuw[:, :, B_DV:].astype(BF16)
    o0_s[...] = iuw[:, :, :B_DV]
    qp_s[...] = (q * eg - iuw[:, :, B_DV:]).astype(BF16)
    egl_s[...] = jnp.broadcast_to(jnp.exp(g_last), (nb, 1, B_DV))

    heads = range(B_HEADS)
    st = [state_ref[hh] for hh in heads]
    for nn in range(n):
        st16 = [s.astype(BF16) for s in st]
        ms = [_dot(m_s[hh * n + nn], st16[hh]) for hh in heads]
        for hh in heads:
            out_s[nn * c:(nn + 1) * c, hh * B_DV:(hh + 1) * B_DV] = _dot(qp_s[hh * n + nn], st16[hh]) + o0_s[hh * n + nn]
        st = [st[hh] * egl_s[hh * n + nn] - ms[hh] + n_s[hh * n + nn] for hh in heads]
    for hh in heads:
        state_ref[hh] = st[hh]
        o_ref[:, hh * B_DV:(hh + 1) * B_DV] = _rms(out_s[:, hh * B_DV:(hh + 1) * B_DV], ng_ref[...])


def _gdn(layer, xb, w, conv_w, a_log, dt_bias, norm_g, batch, seq):
    tile = min(GDN_TILE, seq)
    nt = seq // tile
    hd = B_HEADS * B_DK
    nb = B_HEADS * (tile // B_CHUNK)
    return pl.pallas_call(
        functools.partial(_gdn_kernel, tile=tile),
        grid=(batch, nt),
        in_specs=[pl.BlockSpec((tile, D_MODEL), lambda b, i: (b * nt + i, 0)),
                  *[_layer_spec(a, layer) for a in (w, conv_w, a_log, dt_bias, norm_g)]],
        out_specs=pl.BlockSpec((tile, BRANCH_W), lambda b, i: (b * nt + i, 0)),
        out_shape=jax.ShapeDtypeStruct((batch * seq, BRANCH_W), F32),
        scratch_shapes=[pltpu.VMEM((B_HEADS, B_DK, B_DV), F32),
                        pltpu.VMEM((SUBLANES + tile, 3 * hd), F32),
                        pltpu.VMEM((nb, B_DK, B_DK), BF16),
                        pltpu.VMEM((nb, B_DK, B_DV), F32),
                        pltpu.VMEM((nb, B_CHUNK, B_DK), BF16),
                        pltpu.VMEM((nb, B_CHUNK, B_DV), F32),
                        pltpu.VMEM((nb, 1, B_DV), F32),
                        pltpu.VMEM((tile, hd), F32)],
        compiler_params=_params("arbitrary", "arbitrary"),
        name="gdn",
    )(xb, w, conv_w, a_log, dt_bias, norm_g)


def _hgrn_levels(tile):
    out = []
    lv = tile // 2
    while lv >= HGRN_SUB:
        out.append(lv)
        lv //= 2
    return tuple(out)


def _hgrn_level_map(tile):
    i = np.arange(tile)[:, None]
    j = np.arange(tile)[None, :]
    x = np.bitwise_xor(i, j)
    lev = np.where(j < i, np.floor(np.log2(np.maximum(x, 1))).astype(np.int32), -1)
    return jnp.asarray(lev, jnp.int32)


def _hgrn_kernel(xb_ref, w_ref, lb_ref, ng_ref, lev_ref, tri_ref, o_ref,
                 state_ref, q_s, k_s, v_s, cum_s, acc_s, *, tile):
    hd = C_HEADS * C_DK

    @pl.when(pl.program_id(1) == 0)
    def _():
        state_ref[...] = jnp.zeros_like(state_ref)

    h = _dot(xb_ref[...], w_ref[...])
    lev = lev_ref[...]
    sub_row = lax.broadcasted_iota(jnp.int32, (HGRN_SUB, C_DV), 0)

    lb = lb_ref[...]
    q_all = _silu(h[:, :hd]) * (C_DK ** -0.5)
    sig = _sigmoid(h[:, hd:2 * hd])
    log_f = jnp.log(jnp.maximum(lb + (1.0 - lb) * sig, 1e-30))
    k_all = (1.0 - lb) * (1.0 - sig)
    f_hi = log_f.astype(BF16)
    r1 = log_f - f_hi.astype(F32)
    f_mid = r1.astype(BF16)
    f_lo = (r1 - f_mid.astype(F32)).astype(BF16)
    parts = _dot(tri_ref[...], jnp.concatenate([f_hi, f_mid, f_lo], axis=1))
    cum_all = parts[:, :hd] + parts[:, hd:2 * hd] + parts[:, 2 * hd:]

    for hh in range(C_HEADS):
        sl = slice(hh * C_DK, (hh + 1) * C_DK)
        q = q_all[:, sl]
        k = k_all[:, sl]
        v = h[:, 2 * hd + hh * C_DV:2 * hd + (hh + 1) * C_DV]
        cum = cum_all[:, sl]

        state_t = state_ref[hh]
        out = _dot_nt((q * jnp.exp(cum)).astype(BF16), state_t.astype(BF16))
        cum_last = cum[tile - 1:tile, :]
        k_dec = k * jnp.exp(cum_last - cum)
        v16 = v.astype(BF16)
        state_ref[hh] = state_t * jnp.exp(cum_last) + _dot_tn(v16, k_dec.astype(BF16))

        scores = jnp.zeros((tile, tile), F32)
        for lv in _hgrn_levels(tile):
            blocks = tile // (2 * lv)
            cum3 = cum.reshape(blocks, 2 * lv, C_DK)
            ref = jnp.broadcast_to(cum3[:, lv:lv + 1, :], cum3.shape).reshape(tile, C_DK)
            qt = q * jnp.exp(jnp.minimum(cum - ref, 0.0))
            kt = k * jnp.exp(jnp.minimum(ref - cum, 0.0))
            s_lv = _dot_nt(qt.astype(BF16), kt.astype(BF16))
            scores = jnp.where(lev == int(np.log2(lv)), s_lv, scores)
        acc_s[...] = out + _dot(scores.astype(BF16), v16)

        q_s[...] = q
        k_s[...] = k
        v_s[...] = v
        cum_s[...] = cum

        def group(gi, _):
            base = pl.multiple_of(gi * (HGRN_GROUP * HGRN_SUB), HGRN_GROUP * HGRN_SUB)
            blocks = [pl.ds(base + ss * HGRN_SUB, HGRN_SUB) for ss in range(HGRN_GROUP)]
            qv = [q_s[r, :] for r in blocks]
            cv = [cum_s[r, :] for r in blocks]
            acc = [acc_s[r, :] for r in blocks]
            for j in range(HGRN_SUB):
                for ss in range(HGRN_GROUP):
                    rj = pl.ds(base + ss * HGRN_SUB + j, 1)
                    e = jnp.exp(jnp.minimum(cv[ss] - cum_s[rj, :], 0.0))
                    sc = jnp.sum(qv[ss] * k_s[rj, :] * e, axis=1, keepdims=True)
                    acc[ss] = acc[ss] + jnp.where(sub_row >= j, sc, 0.0) * v_s[rj, :]
            for ss in range(HGRN_GROUP):
                acc_s[blocks[ss], :] = acc[ss]
            return 0

        lax.fori_loop(0, tile // (HGRN_GROUP * HGRN_SUB), group, 0)
        o_ref[:, sl] = _rms(acc_s[...], ng_ref[...])


def _hgrn(layer, xb, w, lower_bound, norm_g, batch, seq):
    tile = min(REC_TILE, seq)
    nt = seq // tile
    lev = _hgrn_level_map(tile)
    tri = jnp.asarray(np.tril(np.ones((tile, tile), np.float32)), BF16)
    return pl.pallas_call(
        functools.partial(_hgrn_kernel, tile=tile),
        grid=(batch, nt),
        in_specs=[pl.BlockSpec((tile, D_MODEL), lambda b, i: (b * nt + i, 0)),
                  *[_layer_spec(a, layer) for a in (w, lower_bound, norm_g)],
                  _const_spec(lev.shape), _const_spec(tri.shape)],
        out_specs=pl.BlockSpec((tile, BRANCH_W), lambda b, i: (b * nt + i, 0)),
        out_shape=jax.ShapeDtypeStruct((batch * seq, BRANCH_W), F32),
        scratch_shapes=[pltpu.VMEM((C_HEADS, C_DV, C_DK), F32)]
                       + [pltpu.VMEM((tile, C_DK), F32) for _ in range(5)],
        compiler_params=_params("arbitrary", "arbitrary"),
        name="hgrn2",
    )(xb, w, lower_bound, norm_g, lev, tri)


def _merge_kernel(x_ref, xb_ref, oa_ref, ob_ref, oc_ref, p_ref, wz_ref, wg_ref, wbr_ref, wout_ref,
                  pgate_ref, pproj_ref, lng_ref, lnb_ref, xo_ref, xbo_ref):
    xb = xb_ref[...]
    z = _dot(xb, wz_ref[...])
    merged = None
    for n, o_ref in enumerate((oa_ref, ob_ref, oc_ref)):
        y = o_ref[...] * _silu(z[:, n * BRANCH_W:(n + 1) * BRANCH_W])
        proj = _dot(y.astype(BF16), wbr_ref[n])
        gate = _sigmoid(_dot(xb, wg_ref[:, n * D_MODEL:(n + 1) * D_MODEL]))
        merged = gate * proj if merged is None else merged + gate * proj
    r = DEEPNORM_ALPHA * x_ref[...] + _dot(merged.astype(BF16), wout_ref[...])
    ple = _dot(p_ref[...].astype(BF16), pproj_ref[...])
    r = r + _sigmoid(_dot(r.astype(BF16), pgate_ref[...])) * ple
    mu = jnp.mean(r, axis=-1, keepdims=True)
    var = jnp.mean(jnp.square(r - mu), axis=-1, keepdims=True)
    xn = (r - mu) * lax.rsqrt(var + NORM_EPS) * lng_ref[...] + lnb_ref[...]
    xo_ref[...] = xn
    xbo_ref[...] = xn.astype(BF16)


def _merge(layer, x, xb, oa, ob, oc, p, wz, wg, wbr, wout, pgate, pproj, lng, lnb):
    t = x.shape[0]
    tm = min(TOKEN_TILE, t)
    row = lambda i: (i, 0)
    return pl.pallas_call(
        _merge_kernel,
        grid=(t // tm,),
        in_specs=[pl.BlockSpec((tm, D_MODEL), row), pl.BlockSpec((tm, D_MODEL), row),
                  pl.BlockSpec((tm, BRANCH_W), row), pl.BlockSpec((tm, BRANCH_W), row),
                  pl.BlockSpec((tm, BRANCH_W), row), pl.BlockSpec((None, tm, PLE_DIM), lambda i: (layer, i, 0)),
                  *[_layer_spec(a, layer) for a in (wz, wg, wbr, wout, pgate, pproj, lng, lnb)]],
        out_specs=[pl.BlockSpec((tm, D_MODEL), row), pl.BlockSpec((tm, D_MODEL), row)],
        out_shape=[jax.ShapeDtypeStruct((t, D_MODEL), F32), jax.ShapeDtypeStruct((t, D_MODEL), BF16)],
        compiler_params=_params("parallel"),
        name="merge",
    )(x, xb, oa, ob, oc, p, wz, wg, wbr, wout, pgate, pproj, lng, lnb)


def _cols(w, idx):
    return w[..., IN_OFFSETS[idx]:IN_OFFSETS[idx + 1]]


def _rot_cols(w):
    half = w.shape[-1] // 2
    return jnp.concatenate([-w[..., half:], w[..., :half]], axis=-1)


def _pad_last(w, left, total):
    return jnp.pad(w, ((0, 0),) * (w.ndim - 1) + ((left, total - left - w.shape[-1]),))


def _mla_weights(w_in, w_uq, w_ukv):
    depth = w_in.shape[0]
    w_kr = _cols(w_in, 2)
    wa = jnp.concatenate([_cols(w_in, 0), _cols(w_in, 1), _pad_last(w_kr, A_NOPE, LANES),
                          _pad_last(_rot_cols(w_kr), A_NOPE, LANES)], axis=-1).astype(BF16)
    uq = w_uq.reshape(depth, A_Q_LORA, A_HEADS, A_NOPE + A_ROPE)
    wq = _pad_last(uq, 0, A_HEAD_PAD).reshape(depth, A_Q_LORA, A_HEADS * A_HEAD_PAD).astype(BF16)
    uq_rot = jnp.concatenate([jnp.zeros_like(uq[..., :A_NOPE]), _rot_cols(uq[..., A_NOPE:])], axis=-1)
    wqr = _pad_last(uq_rot, 0, A_HEAD_PAD).reshape(depth, A_Q_LORA, A_HEADS * A_HEAD_PAD).astype(BF16)
    ukv = w_ukv.reshape(depth, A_KV_LORA, A_HEADS, A_NOPE + A_VDIM)
    wk = _pad_last(ukv[..., :A_NOPE], 0, A_HEAD_PAD).reshape(depth, A_KV_LORA, A_HEADS * A_HEAD_PAD).astype(BF16)
    wv = _pad_last(ukv[..., A_NOPE:], 0, A_HEAD_PAD).reshape(depth, A_KV_LORA, A_HEADS * A_HEAD_PAD).astype(BF16)
    return wa, wq, wqr, wk, wv


def _rope_tables(positions):
    inv = ROPE_THETA ** (-jnp.arange(0, A_ROPE, 2, dtype=F32) / A_ROPE)
    ang = positions.astype(F32).reshape(-1, 1) * inv
    cos, sin = lax.optimization_barrier((jnp.cos(ang), jnp.sin(ang)))
    place = lambda t: _pad_last(jnp.concatenate([t, t], axis=-1), A_NOPE, LANES)
    return place(cos), place(sin)


def _lower_bounds(logits):
    pr = jax.nn.softmax(logits.astype(F32), axis=0)
    return jnp.clip(jnp.cumsum(pr, axis=0) - pr[0:1], 0.0, 1.0 - 1e-6)


def _rows(v):
    return v.astype(F32)[:, None, :]


def kernel(x, p, positions, w_in, a_q_norm, a_w_uq, a_kv_norm, a_w_ukv, b_conv, b_a_log, b_dt_bias, b_norm,
           c_lb_logits, c_norm, w_branch, w_out, ple_proj, ple_gate, ln_g, ln_b):
    batch, seq, _ = x.shape
    t = batch * seq
    depth = w_in.shape[0]
    cos_t, sin_t = _rope_tables(positions)

    wa, wq, wqr, wk, wv = _mla_weights(w_in, a_w_uq, a_w_ukv)
    gate_cols = jnp.concatenate([_cols(w_in, 5), _cols(w_in, 6)], axis=-1)
    wb = jnp.concatenate([_cols(w_in, 4), _pad_last(gate_cols, 0, LANES)], axis=-1).astype(BF16)
    wc = jnp.concatenate([_cols(w_in, 8), _cols(w_in, 9), _cols(w_in, 10)], axis=-1).astype(BF16)
    wz = jnp.concatenate([_cols(w_in, 3), _cols(w_in, 7), _cols(w_in, 11)], axis=-1).astype(BF16)
    wg = _cols(w_in, 12).astype(BF16)
    wbr, wout, pgate, pproj = (w.astype(BF16) for w in (w_branch, w_out, ple_gate, ple_proj))
    qg, kvg, ng_b, ng_c, lng, lnb = (_rows(v) for v in (a_q_norm, a_kv_norm, b_norm, c_norm, ln_g, ln_b))
    conv_w = b_conv.reshape(depth, B_CONV, -1)
    a_log = _rows(_pad_last(b_a_log, 0, LANES))
    dt_bias = _rows(_pad_last(b_dt_bias, 0, LANES))
    lower_bounds = _rows(_lower_bounds(c_lb_logits))
    p_rows = p.reshape(depth, t, PLE_DIM)

    xf = x.reshape(t, D_MODEL)
    xb = xf.astype(BF16)
    for i in range(depth):
        q, k, v = _mla_prep(i, xb, cos_t, sin_t, wa, qg, kvg, wq, wqr, wk, wv)
        o_a = _mla_flash(q, k, v, batch, seq)
        o_b = _gdn(i, xb, wb, conv_w, a_log, dt_bias, ng_b, batch, seq)
        o_c = _hgrn(i, xb, wc, lower_bounds, ng_c, batch, seq)
        xf, xb = _merge(i, xf, xb, o_a, o_b, o_c, p_rows, wz, wg, wbr, wout, pgate, pproj, lng, lnb)
    return xf.reshape(batch, seq, D_MODEL)
```

```python
import functools

import numpy as np
import jax
import jax.numpy as jnp
from jax import lax
from jax.experimental import pallas as pl
from jax.experimental.pallas import tpu as pltpu

F32 = jnp.float32
BF16 = jnp.bfloat16

D_MODEL = 1024
DEPTH = 4
PLE_DIM = 256
BRANCH_W = 512
N_BRANCH = 3
A_HEADS = 8
A_NOPE = 64
A_ROPE = 32
A_VDIM = 64
A_Q_LORA = 256
A_KV_LORA = 128
A_SCALE = (A_NOPE + A_ROPE) ** -0.5
ROPE_THETA = 10000.0
B_HEADS = 4
B_DK = 128
B_DV = 128
B_CONV = 4
B_CHUNK = 64
C_HEADS = 4
C_DK = 128
C_DV = 128
NORM_EPS = 1e-6
MASK_VALUE = -1e30
DEEPNORM_ALPHA = (2.0 * DEPTH) ** 0.25
LOG2_E = 1.4426950408889634

IN_SPLITS = (A_Q_LORA, A_KV_LORA, A_ROPE, BRANCH_W, 3 * B_HEADS * B_DK, B_HEADS, B_HEADS, BRANCH_W,
             C_HEADS * C_DK, C_HEADS * C_DK, C_HEADS * C_DV, BRANCH_W, N_BRANCH * D_MODEL)
IN_OFFSETS = tuple(int(o) for o in np.cumsum((0,) + IN_SPLITS))

LANES = 128
SUBLANES = 8
A_HEAD_PAD = LANES
VMEM_LIMIT = 56 * 1024 * 1024

TOKEN_TILE = 512
ATTN_TILE = 512
ATTN_Q_TILE = 512
ATTN_GROUP = 4
REC_TILE = 256
GDN_TILE = 512
HGRN_SUB = SUBLANES
HGRN_GROUP = 16
HGRN_DENSE_MIN = 32


def _sigmoid(x):
    return 1.0 / (1.0 + jnp.exp(-x))


def _silu(x):
    return x * _sigmoid(x)


def _softplus(x):
    return jnp.maximum(x, 0.0) + jnp.log(1.0 + jnp.exp(-jnp.abs(x)))


def _rms(x, g):
    return x * lax.rsqrt(jnp.mean(x * x, axis=-1, keepdims=True) + NORM_EPS) * g


def _dot(a, b):
    return jnp.dot(a, b, preferred_element_type=F32)


def _dot_nt(a, b):
    return lax.dot_general(a, b, (((1,), (1,)), ((), ())), preferred_element_type=F32)


def _dot_tn(a, b):
    return lax.dot_general(a, b, (((0,), (0,)), ((), ())), preferred_element_type=F32)


def _const_spec(shape):
    nd = len(shape)
    return pl.BlockSpec(shape, lambda *_: (0,) * nd, pipeline_mode=pl.Buffered(1))


def _layer_spec(arr, layer):
    nd = arr.ndim - 1
    return pl.BlockSpec((None,) + arr.shape[1:], lambda *_: (layer,) + (0,) * nd, pipeline_mode=pl.Buffered(1))


def _params(*sem):
    return pltpu.CompilerParams(dimension_semantics=sem, vmem_limit_bytes=VMEM_LIMIT)


def _mla_prep_kernel(xb_ref, cos_ref, sin_ref, wa_ref, qg_ref, kvg_ref, wq_ref, wqr_ref, wk_ref, wv_ref,
                     q_ref, k_ref, v_ref):
    h = _dot(xb_ref[...], wa_ref[...])
    cq = h[:, :A_Q_LORA]
    ckv = h[:, A_Q_LORA:A_Q_LORA + A_KV_LORA]
    kr = h[:, A_Q_LORA + A_KV_LORA:A_Q_LORA + A_KV_LORA + LANES]
    kr_rot = h[:, A_Q_LORA + A_KV_LORA + LANES:]
    cqn = _rms(cq, qg_ref[...]).astype(BF16)
    ckvn = _rms(ckv, kvg_ref[...]).astype(BF16)
    cos = cos_ref[...]
    sin = sin_ref[...]
    lane = lax.broadcasted_iota(jnp.int32, cos.shape, 1)
    cq_tab = jnp.where(lane < A_NOPE, 1.0, cos) * (A_SCALE * LOG2_E)
    sq_tab = sin * (A_SCALE * LOG2_E)
    cq_tab = jnp.concatenate([cq_tab] * A_HEADS, axis=1)
    sq_tab = jnp.concatenate([sq_tab] * A_HEADS, axis=1)
    q = _dot(cqn, wq_ref[...]) * cq_tab + _dot(cqn, wqr_ref[...]) * sq_tab
    q_ref[...] = q.astype(BF16)
    k_rope = kr * cos + kr_rot * sin
    k = _dot(ckvn, wk_ref[...]) + jnp.concatenate([k_rope] * A_HEADS, axis=1)
    k_ref[...] = k.astype(BF16)
    v = _dot(ckvn, wv_ref[...])
    vlane = lax.broadcasted_iota(jnp.int32, v.shape, 1) % A_HEAD_PAD
    v_ref[...] = jnp.where(vlane == A_VDIM, 1.0, v).astype(BF16)


def _mla_prep(layer, xb, cos_t, sin_t, wa, qg, kvg, wq, wqr, wk, wv):
    t = xb.shape[0]
    tm = min(TOKEN_TILE, t)
    row = lambda i: (i, 0)
    hp = A_HEADS * A_HEAD_PAD
    return pl.pallas_call(
        _mla_prep_kernel,
        grid=(t // tm,),
        in_specs=[pl.BlockSpec((tm, D_MODEL), row), pl.BlockSpec((tm, LANES), row), pl.BlockSpec((tm, LANES), row),
                  *[_layer_spec(a, layer) for a in (wa, qg, kvg, wq, wqr, wk, wv)]],
        out_specs=[pl.BlockSpec((tm, hp), row), pl.BlockSpec((tm, hp), row), pl.BlockSpec((tm, hp), row)],
        out_shape=[jax.ShapeDtypeStruct((t, hp), BF16)] * 3,
        compiler_params=_params("parallel"),
        name="mla_prep",
    )(xb, cos_t, sin_t, wa, qg, kvg, wq, wqr, wk, wv)


def _flash_kernel(q_ref, k_ref, v_ref, o_ref, *, tq, tk):
    qi = pl.program_id(2)
    ratio = tq // tk
    qs = (q_ref[:, :A_HEAD_PAD], q_ref[:, A_HEAD_PAD:])
    heads = tuple(slice(hh * A_HEAD_PAD, (hh + 1) * A_HEAD_PAD) for hh in range(2))
    row = lax.broadcasted_iota(jnp.int32, (tq, tk), 0)
    col = lax.broadcasted_iota(jnp.int32, (tq, tk), 1)

    def scores(j):
        kk = k_ref[pl.ds(pl.multiple_of(j * tk, tk), tk), :]
        return [_dot_nt(qs[hh], kk[:, heads[hh]]) for hh in range(2)]

    def update(carry, j, s, diag):
        vv = v_ref[pl.ds(pl.multiple_of(j * tk, tk), tk), :]
        out = []
        for hh in range(2):
            m, acc = carry[2 * hh:2 * hh + 2]
            sh = s[hh] if diag is None else jnp.where(col + diag * tk <= row, s[hh], MASK_VALUE)
            m_new = jnp.maximum(m, jnp.max(sh, axis=1, keepdims=True))
            p = jnp.exp2(sh - m_new)
            acc = jnp.exp2(m - m_new) * acc + _dot(p.astype(BF16), vv[:, heads[hh]])
            out += [m_new, acc]
        return tuple(out)

    def group(i, carry):
        js = [ATTN_GROUP * i + t for t in range(ATTN_GROUP)]
        ss = [scores(j) for j in js]
        for j, s in zip(js, ss):
            carry = update(carry, j, s, None)
        return carry

    carry = (jnp.full((tq, 1), -jnp.inf, F32), jnp.zeros((tq, A_HEAD_PAD), F32)) * 2
    full = qi * ratio
    groups = full // ATTN_GROUP
    carry = lax.fori_loop(0, groups, group, carry)
    if ratio % ATTN_GROUP:
        carry = lax.fori_loop(groups * ATTN_GROUP, full, lambda j, c: update(c, j, scores(j), None), carry)
    for d in range(ratio):
        carry = update(carry, full + d, scores(full + d), d)
    _, acc0, _, acc1 = carry
    o0 = acc0 / acc0[:, A_VDIM:A_VDIM + 1]
    o1 = acc1 / acc1[:, A_VDIM:A_VDIM + 1]
    lane = lax.broadcasted_iota(jnp.int32, o0.shape, 1)
    o_ref[...] = jnp.where(lane < A_VDIM, o0, pltpu.roll(o1, A_VDIM, 1))


def _mla_flash(q, k, v, batch, seq):
    tk = min(ATTN_TILE, seq)
    tq = min(ATTN_Q_TILE, seq)
    nq = seq // tq
    return pl.pallas_call(
        functools.partial(_flash_kernel, tq=tq, tk=tk),
        grid=(batch, A_HEADS // 2, nq),
        in_specs=[pl.BlockSpec((tq, 2 * A_HEAD_PAD), lambda b, h, i: (b * nq + i, h)),
                  pl.BlockSpec((seq, 2 * A_HEAD_PAD), lambda b, h, i: (b, h)),
                  pl.BlockSpec((seq, 2 * A_HEAD_PAD), lambda b, h, i: (b, h))],
        out_specs=pl.BlockSpec((tq, 2 * A_VDIM), lambda b, h, i: (b * nq + i, h)),
        out_shape=jax.ShapeDtypeStruct((batch * seq, BRANCH_W), F32),
        compiler_params=_params("parallel", "parallel", "arbitrary"),
        name="mla_flash",
    )(q, k, v)


def _cumsum_rows(x, period):
    pos = lax.broadcasted_iota(jnp.int32, x.shape, 0) % period
    s = 1
    while s < period:
        x = x + jnp.where(pos >= s, pltpu.roll(x, s, 0), 0.0)
        s *= 2
    return x


def _bdot(a, b):
    return lax.dot_general(a, b, (((2,), (1,)), ((0,), (0,))), preferred_element_type=F32)


def _bdot_nt(a, b):
    return lax.dot_general(a, b, (((2,), (2,)), ((0,), (0,))), preferred_element_type=F32)


def _unit_lower_inverse(low):
    c = low.shape[-1]
    eye = (lax.broadcasted_iota(jnp.int32, (c, c), 0) == lax.broadcasted_iota(jnp.int32, (c, c), 1)).astype(F32)
    inv = eye - low
    pw16 = low.astype(BF16)
    k = 2
    while k < c:
        pw16 = _bdot(pw16, pw16).astype(BF16)
        inv = inv + _bdot(inv.astype(BF16), pw16)
        k *= 2
    return inv


def _gdn_kernel(xb_ref, w_ref, convw_ref, alog_ref, dtb_ref, ng_ref, o_ref,
                state_ref, xbuf_ref, m_s, n_s, qp_s, o0_s, egl_s, out_s, *, tile):
    c = B_CHUNK
    n = tile // c
    nb = B_HEADS * n
    hd = B_HEADS * B_DK

    @pl.when(pl.program_id(1) == 0)
    def _():
        state_ref[...] = jnp.zeros_like(state_ref)
        xbuf_ref[0:SUBLANES, :] = jnp.zeros((SUBLANES, 3 * hd), F32)

    h = _dot(xb_ref[...], w_ref[...])
    xbuf_ref[SUBLANES:, :] = h[:, :3 * hd]
    conv = h[:, :3 * hd] * convw_ref[B_CONV - 1:B_CONV, :]
    for s in range(1, B_CONV):
        conv = conv + xbuf_ref[pl.ds(SUBLANES - s, tile), :] * convw_ref[B_CONV - 1 - s:B_CONV - s, :]
    xbuf_ref[0:SUBLANES, :] = h[tile - SUBLANES:, :3 * hd]
    qkv = _silu(conv)

    gates = h[:, 3 * hd:]
    g_all = -jnp.exp(alog_ref[...]) * _softplus(gates + dtb_ref[...])
    gcum = _cumsum_rows(g_all, c)
    beta_all = _sigmoid(gates)

    def per_head(fn):
        return jnp.concatenate([fn(hh) for hh in range(B_HEADS)], axis=0)

    def unit(x):
        return x * lax.rsqrt(jnp.sum(x * x, axis=-1, keepdims=True) + NORM_EPS)

    q = per_head(lambda hh: (unit(qkv[:, hh * B_DK:(hh + 1) * B_DK]) * (B_DK ** -0.5)).reshape(n, c, B_DK))
    k = per_head(lambda hh: unit(qkv[:, hd + hh * B_DK:hd + (hh + 1) * B_DK]).reshape(n, c, B_DK))
    v = per_head(lambda hh: qkv[:, 2 * hd + hh * B_DV:2 * hd + (hh + 1) * B_DV].reshape(n, c, B_DV))
    gc = per_head(lambda hh: gcum[:, hh:hh + 1].reshape(n, c, 1))
    beta = per_head(lambda hh: beta_all[:, B_HEADS + hh:B_HEADS + hh + 1].reshape(n, c, 1))

    ri = lax.broadcasted_iota(jnp.int32, (c, c), 0)
    ci = lax.broadcasted_iota(jnp.int32, (c, c), 1)
    tril = ci <= ri
    gcb = jnp.broadcast_to(gc, (nb, c, c))
    diff = gcb - jnp.swapaxes(gcb, 1, 2)
    decay = jnp.where(tril, jnp.exp(jnp.where(tril, diff, 0.0)), 0.0)
    k16 = k.astype(BF16)
    k_beta = k * beta
    low = jnp.where(ci < ri, _bdot_nt(k_beta.astype(BF16), k16) * decay, 0.0)
    inv = _unit_lower_inverse(low)
    eg = jnp.exp(gc)
    rhs = jnp.concatenate([v * beta, k_beta * eg], axis=2)
    uw16 = _bdot(inv.astype(BF16), rhs.astype(BF16)).astype(BF16)
    intra16 = (_bdot_nt(q.astype(BF16), k16) * decay).astype(BF16)
    g_last = gc[:, c - 1:c, :]
    kd_t = jnp.swapaxes(k * jnp.exp(g_last - gc), 1, 2).astype(BF16)
    kuw = _bdot(kd_t, uw16)
    iuw = _bdot(intra16, uw16)
    n_s[...] = kuw[:, :, :B_DV]
    m_s[...] = kuw[:, :, B_DV:].astype(BF16)
    o0_s[...] = iuw[:, :, :B_DV]
    qp_s[...] = (q * eg - iuw[:, :, B_DV:]).astype(BF16)
    egl_s[...] = jnp.broadcast_to(jnp.exp(g_last), (nb, 1, B_DV))

    heads = range(B_HEADS)
    st = [state_ref[hh] for hh in heads]
    for nn in range(n):
        st16 = [s.astype(BF16) for s in st]
        ms = [_dot(m_s[hh * n + nn], st16[hh]) for hh in heads]
        for hh in heads:
            out_s[nn * c:(nn + 1) * c, hh * B_DV:(hh + 1) * B_DV] = _dot(qp_s[hh * n + nn], st16[hh]) + o0_s[hh * n + nn]
        st = [st[hh] * egl_s[hh * n + nn] - ms[hh] + n_s[hh * n + nn] for hh in heads]
    for hh in heads:
        state_ref[hh] = st[hh]
        o_ref[:, hh * B_DV:(hh + 1) * B_DV] = _rms(out_s[:, hh * B_DV:(hh + 1) * B_DV], ng_ref[...])


def _gdn(layer, xb, w, conv_w, a_log, dt_bias, norm_g, batch, seq):
    tile = min(GDN_TILE, seq)
    nt = seq // tile
    hd = B_HEADS * B_DK
    nb = B_HEADS * (tile // B_CHUNK)
    return pl.pallas_call(
        functools.partial(_gdn_kernel, tile=tile),
        grid=(batch, nt),
        in_specs=[pl.BlockSpec((tile, D_MODEL), lambda b, i: (b * nt + i, 0)),
                  *[_layer_spec(a, layer) for a in (w, conv_w, a_log, dt_bias, norm_g)]],
        out_specs=pl.BlockSpec((tile, BRANCH_W), lambda b, i: (b * nt + i, 0)),
        out_shape=jax.ShapeDtypeStruct((batch * seq, BRANCH_W), F32),
        scratch_shapes=[pltpu.VMEM((B_HEADS, B_DK, B_DV), F32),
                        pltpu.VMEM((SUBLANES + tile, 3 * hd), F32),
                        pltpu.VMEM((nb, B_DK, B_DK), BF16),
                        pltpu.VMEM((nb, B_DK, B_DV), F32),
                        pltpu.VMEM((nb, B_CHUNK, B_DK), BF16),
                        pltpu.VMEM((nb, B_CHUNK, B_DV), F32),
                        pltpu.VMEM((nb, 1, B_DV), F32),
                        pltpu.VMEM((tile, hd), F32)],
        compiler_params=_params("arbitrary", "arbitrary"),
        name="gdn",
    )(xb, w, conv_w, a_log, dt_bias, norm_g)


def _hgrn_levels(tile):
    out = []
    lv = tile // 2
    while lv >= HGRN_SUB:
        out.append(lv)
        lv //= 2
    return tuple(out)


def _hgrn_level_map(tile):
    i = np.arange(tile)[:, None]
    j = np.arange(tile)[None, :]
    x = np.bitwise_xor(i, j)
    lev = np.where(j < i, np.floor(np.log2(np.maximum(x, 1))).astype(np.int32), -1)
    return jnp.asarray(lev, jnp.int32)


def _hgrn_kernel(xb_ref, w_ref, lb_ref, ng_ref, lev_ref, tri_ref, o_ref,
                 state_ref, q_s, k_s, v_s, cum_s, acc_s, *, tile):
    hd = C_HEADS * C_DK

    @pl.when(pl.program_id(1) == 0)
    def _():
        state_ref[...] = jnp.zeros_like(state_ref)

    h = _dot(xb_ref[...], w_ref[...])
    lev = lev_ref[...]
    sub_row = lax.broadcasted_iota(jnp.int32, (HGRN_SUB, C_DV), 0)

    lb = lb_ref[...]
    q_all = _silu(h[:, :hd]) * (C_DK ** -0.5)
    sig = _sigmoid(h[:, hd:2 * hd])
    log_f = jnp.log(jnp.maximum(lb + (1.0 - lb) * sig, 1e-30))
    k_all = (1.0 - lb) * (1.0 - sig)
    f_hi = log_f.astype(BF16)
    r1 = log_f - f_hi.astype(F32)
    f_mid = r1.astype(BF16)
    f_lo = (r1 - f_mid.astype(F32)).astype(BF16)
    parts = _dot(tri_ref[...], jnp.concatenate([f_hi, f_mid, f_lo], axis=1))
    cum_all = parts[:, :hd] + parts[:, hd:2 * hd] + parts[:, 2 * hd:]

    heads = range(C_HEADS)
    cols = [slice(hh * C_DK, (hh + 1) * C_DK) for hh in heads]
    q = [q_all[:, sl] for sl in cols]
    k = [k_all[:, sl] for sl in cols]
    cum = [cum_all[:, sl] for sl in cols]
    v16 = [h[:, 2 * hd + hh * C_DV:2 * hd + (hh + 1) * C_DV].astype(BF16) for hh in heads]
    for hh in heads:
        q_s[hh] = q[hh]
        k_s[hh] = k[hh]
        v_s[hh] = h[:, 2 * hd + hh * C_DV:2 * hd + (hh + 1) * C_DV]
        cum_s[hh] = cum[hh]

    state_t = [state_ref[hh] for hh in heads]
    for hh in heads:
        acc_s[hh] = _dot_nt((q[hh] * jnp.exp(cum[hh])).astype(BF16), state_t[hh].astype(BF16))
    for hh in heads:
        cum_last = cum[hh][tile - 1:tile, :]
        k_dec = k[hh] * jnp.exp(cum_last - cum[hh])
        state_ref[hh] = state_t[hh] * jnp.exp(cum_last) + _dot_tn(v16[hh], k_dec.astype(BF16))

    for lv in _hgrn_levels(tile):
        if lv < HGRN_DENSE_MIN:
            continue
        blocks = tile // (2 * lv)
        shape = (blocks, 2 * lv, C_DK)
        s_lv = []
        for hh in heads:
            q3, k3, c3 = q[hh].reshape(shape), k[hh].reshape(shape), cum[hh].reshape(shape)
            ref = c3[:, lv:lv + 1, :]
            qt = q3[:, lv:, :] * jnp.exp(c3[:, lv:, :] - ref)
            kt = k3[:, :lv, :] * jnp.exp(ref - c3[:, :lv, :])
            s_lv.append(_bdot_nt(qt.astype(BF16), kt.astype(BF16)).astype(BF16))
        for hh in heads:
            o_lv = _bdot(s_lv[hh], v16[hh].reshape(shape)[:, :lv, :])
            for bb in range(blocks):
                acc_s[hh, bb * 2 * lv + lv:(bb + 1) * 2 * lv, :] += o_lv[bb]
    span = 2 * HGRN_DENSE_MIN
    shape = (tile // span, span, C_DK)
    lev_s = lev[:span, :span]
    scores = [jnp.zeros((tile // span, span, span), F32) for _ in heads]
    for lv in _hgrn_levels(tile):
        if lv >= HGRN_DENSE_MIN:
            continue
        for hh in heads:
            c3 = cum[hh].reshape(shape)
            c4 = cum[hh].reshape(tile // (2 * lv), 2 * lv, C_DK)
            ref = jnp.broadcast_to(c4[:, lv:lv + 1, :], c4.shape).reshape(shape)
            qt = q[hh].reshape(shape) * jnp.exp(jnp.minimum(c3 - ref, 0.0))
            kt = k[hh].reshape(shape) * jnp.exp(jnp.minimum(ref - c3, 0.0))
            s_lv = _bdot_nt(qt.astype(BF16), kt.astype(BF16))
            scores[hh] = jnp.where(lev_s == int(np.log2(lv)), s_lv, scores[hh])
    for hh in heads:
        acc_s[hh] += _bdot(scores[hh].astype(BF16), v16[hh].reshape(shape)).reshape(tile, C_DV)

    for hh in heads:
        def group(gi, _):
            base = pl.multiple_of(gi * (HGRN_GROUP * HGRN_SUB), HGRN_GROUP * HGRN_SUB)
            blocks = [pl.ds(base + ss * HGRN_SUB, HGRN_SUB) for ss in range(HGRN_GROUP)]
            qv = [q_s[hh, r, :] for r in blocks]
            cv = [cum_s[hh, r, :] for r in blocks]
            acc = [acc_s[hh, r, :] for r in blocks]
            for j in range(HGRN_SUB):
                for ss in range(HGRN_GROUP):
                    rj = pl.ds(base + ss * HGRN_SUB + j, 1)
                    e = jnp.exp(jnp.minimum(cv[ss] - cum_s[hh, rj, :], 0.0))
                    sc = jnp.sum(qv[ss] * k_s[hh, rj, :] * e, axis=1, keepdims=True)
                    acc[ss] = acc[ss] + jnp.where(sub_row >= j, sc, 0.0) * v_s[hh, rj, :]
            for ss in range(HGRN_GROUP):
                acc_s[hh, blocks[ss], :] = acc[ss]
            return 0

        lax.fori_loop(0, tile // (HGRN_GROUP * HGRN_SUB), group, 0)
        o_ref[:, cols[hh]] = _rms(acc_s[hh], ng_ref[...])


def _hgrn(layer, xb, w, lower_bound, norm_g, batch, seq):
    tile = min(REC_TILE, seq)
    nt = seq // tile
    lev = _hgrn_level_map(tile)
    tri = jnp.asarray(np.tril(np.ones((tile, tile), np.float32)), BF16)
    return pl.pallas_call(
        functools.partial(_hgrn_kernel, tile=tile),
        grid=(batch, nt),
        in_specs=[pl.BlockSpec((tile, D_MODEL), lambda b, i: (b * nt + i, 0)),
                  *[_layer_spec(a, layer) for a in (w, lower_bound, norm_g)],
                  _const_spec(lev.shape), _const_spec(tri.shape)],
        out_specs=pl.BlockSpec((tile, BRANCH_W), lambda b, i: (b * nt + i, 0)),
        out_shape=jax.ShapeDtypeStruct((batch * seq, BRANCH_W), F32),
        scratch_shapes=[pltpu.VMEM((C_HEADS, C_DV, C_DK), F32)]
                       + [pltpu.VMEM((C_HEADS, tile, C_DK), F32) for _ in range(5)],
        compiler_params=_params("arbitrary", "arbitrary"),
        name="hgrn2",
    )(xb, w, lower_bound, norm_g, lev, tri)


def _merge_kernel(x_ref, xb_ref, oa_ref, ob_ref, oc_ref, p_ref, wz_ref, wg_ref, wbr_ref, wout_ref,
                  pgate_ref, pproj_ref, lng_ref, lnb_ref, xo_ref, xbo_ref):
    xb = xb_ref[...]
    z = _dot(xb, wz_ref[...])
    merged = None
    for n, o_ref in enumerate((oa_ref, ob_ref, oc_ref)):
        y = o_ref[...] * _silu(z[:, n * BRANCH_W:(n + 1) * BRANCH_W])
        proj = _dot(y.astype(BF16), wbr_ref[n])
        gate = _sigmoid(_dot(xb, wg_ref[:, n * D_MODEL:(n + 1) * D_MODEL]))
        merged = gate * proj if merged is None else merged + gate * proj
    r = DEEPNORM_ALPHA * x_ref[...] + _dot(merged.astype(BF16), wout_ref[...])
    ple = _dot(p_ref[...].astype(BF16), pproj_ref[...])
    r = r + _sigmoid(_dot(r.astype(BF16), pgate_ref[...])) * ple
    mu = jnp.mean(r, axis=-1, keepdims=True)
    var = jnp.mean(jnp.square(r - mu), axis=-1, keepdims=True)
    xn = (r - mu) * lax.rsqrt(var + NORM_EPS) * lng_ref[...] + lnb_ref[...]
    xo_ref[...] = xn
    xbo_ref[...] = xn.astype(BF16)


def _merge(layer, x, xb, oa, ob, oc, p, wz, wg, wbr, wout, pgate, pproj, lng, lnb):
    t = x.shape[0]
    tm = min(TOKEN_TILE, t)
    row = lambda i: (i, 0)
    return pl.pallas_call(
        _merge_kernel,
        grid=(t // tm,),
        in_specs=[pl.BlockSpec((tm, D_MODEL), row), pl.BlockSpec((tm, D_MODEL), row),
                  pl.BlockSpec((tm, BRANCH_W), row), pl.BlockSpec((tm, BRANCH_W), row),
                  pl.BlockSpec((tm, BRANCH_W), row), pl.BlockSpec((None, tm, PLE_DIM), lambda i: (layer, i, 0)),
                  *[_layer_spec(a, layer) for a in (wz, wg, wbr, wout, pgate, pproj, lng, lnb)]],
        out_specs=[pl.BlockSpec((tm, D_MODEL), row), pl.BlockSpec((tm, D_MODEL), row)],
        out_shape=[jax.ShapeDtypeStruct((t, D_MODEL), F32), jax.ShapeDtypeStruct((t, D_MODEL), BF16)],
        compiler_params=_params("parallel"),
        name="merge",
    )(x, xb, oa, ob, oc, p, wz, wg, wbr, wout, pgate, pproj, lng, lnb)


def _cols(w, idx):
    return w[..., IN_OFFSETS[idx]:IN_OFFSETS[idx + 1]]


def _rot_cols(w):
    half = w.shape[-1] // 2
    return jnp.concatenate([-w[..., half:], w[..., :half]], axis=-1)


def _pad_last(w, left, total):
    return jnp.pad(w, ((0, 0),) * (w.ndim - 1) + ((left, total - left - w.shape[-1]),))


def _mla_weights(w_in, w_uq, w_ukv):
    depth = w_in.shape[0]
    w_kr = _cols(w_in, 2)
    wa = jnp.concatenate([_cols(w_in, 0), _cols(w_in, 1), _pad_last(w_kr, A_NOPE, LANES),
                          _pad_last(_rot_cols(w_kr), A_NOPE, LANES)], axis=-1).astype(BF16)
    uq = w_uq.reshape(depth, A_Q_LORA, A_HEADS, A_NOPE + A_ROPE)
    wq = _pad_last(uq, 0, A_HEAD_PAD).reshape(depth, A_Q_LORA, A_HEADS * A_HEAD_PAD).astype(BF16)
    uq_rot = jnp.concatenate([jnp.zeros_like(uq[..., :A_NOPE]), _rot_cols(uq[..., A_NOPE:])], axis=-1)
    wqr = _pad_last(uq_rot, 0, A_HEAD_PAD).reshape(depth, A_Q_LORA, A_HEADS * A_HEAD_PAD).astype(BF16)
    ukv = w_ukv.reshape(depth, A_KV_LORA, A_HEADS, A_NOPE + A_VDIM)
    wk = _pad_last(ukv[..., :A_NOPE], 0, A_HEAD_PAD).reshape(depth, A_KV_LORA, A_HEADS * A_HEAD_PAD).astype(BF16)
    wv = _pad_last(ukv[..., A_NOPE:], 0, A_HEAD_PAD).reshape(depth, A_KV_LORA, A_HEADS * A_HEAD_PAD).astype(BF16)
    return wa, wq, wqr, wk, wv


def _rope_tables(positions):
    inv = ROPE_THETA ** (-jnp.arange(0, A_ROPE, 2, dtype=F32) / A_ROPE)
    ang = positions.astype(F32).reshape(-1, 1) * inv
    cos, sin = lax.optimization_barrier((jnp.cos(ang), jnp.sin(ang)))
    place = lambda t: _pad_last(jnp.concatenate([t, t], axis=-1), A_NOPE, LANES)
    return place(cos), place(sin)


def _lower_bounds(logits):
    pr = jax.nn.softmax(logits.astype(F32), axis=0)
    return jnp.clip(jnp.cumsum(pr, axis=0) - pr[0:1], 0.0, 1.0 - 1e-6)


def _rows(v):
    return v.astype(F32)[:, None, :]


def kernel(x, p, positions, w_in, a_q_norm, a_w_uq, a_kv_norm, a_w_ukv, b_conv, b_a_log, b_dt_bias, b_norm,
           c_lb_logits, c_norm, w_branch, w_out, ple_proj, ple_gate, ln_g, ln_b):
    batch, seq, _ = x.shape
    t = batch * seq
    depth = w_in.shape[0]
    cos_t, sin_t = _rope_tables(positions)

    wa, wq, wqr, wk, wv = _mla_weights(w_in, a_w_uq, a_w_ukv)
    gate_cols = jnp.concatenate([_cols(w_in, 5), _cols(w_in, 6)], axis=-1)
    wb = jnp.concatenate([_cols(w_in, 4), _pad_last(gate_cols, 0, LANES)], axis=-1).astype(BF16)
    wc = jnp.concatenate([_cols(w_in, 8), _cols(w_in, 9), _cols(w_in, 10)], axis=-1).astype(BF16)
    wz = jnp.concatenate([_cols(w_in, 3), _cols(w_in, 7), _cols(w_in, 11)], axis=-1).astype(BF16)
    wg = _cols(w_in, 12).astype(BF16)
    wbr, wout, pgate, pproj = (w.astype(BF16) for w in (w_branch, w_out, ple_gate, ple_proj))
    qg, kvg, ng_b, ng_c, lng, lnb = (_rows(v) for v in (a_q_norm, a_kv_norm, b_norm, c_norm, ln_g, ln_b))
    conv_w = b_conv.reshape(depth, B_CONV, -1)
    a_log = _rows(_pad_last(b_a_log, 0, LANES))
    dt_bias = _rows(_pad_last(b_dt_bias, 0, LANES))
    lower_bounds = _rows(_lower_bounds(c_lb_logits))
    p_rows = p.reshape(depth, t, PLE_DIM)

    xf = x.reshape(t, D_MODEL)
    xb = xf.astype(BF16)
    for i in range(depth):
        q, k, v = _mla_prep(i, xb, cos_t, sin_t, wa, qg, kvg, wq, wqr, wk, wv)
        o_a = _mla_flash(q, k, v, batch, seq)
        o_b = _gdn(i, xb, wb, conv_w, a_log, dt_bias, ng_b, batch, seq)
        o_c = _hgrn(i, xb, wc, lower_bounds, ng_c, batch, seq)
        xf, xb = _merge(i, xf, xb, o_a, o_b, o_c, p_rows, wz, wg, wbr, wout, pgate, pproj, lng, lnb)
    return xf.reshape(batch, seq, D_MODEL)
```

```python
import functools

import numpy as np
import jax
import jax.numpy as jnp
from jax import lax
from jax.experimental import pallas as pl
from jax.experimental.pallas import tpu as pltpu

F32 = jnp.float32
BF16 = jnp.bfloat16

D_MODEL = 1024
DEPTH = 4
PLE_DIM = 256
BRANCH_W = 512
N_BRANCH = 3
A_HEADS = 8
A_NOPE = 64
A_ROPE = 32
A_VDIM = 64
A_Q_LORA = 256
A_KV_LORA = 128
A_SCALE = (A_NOPE + A_ROPE) ** -0.5
ROPE_THETA = 10000.0
B_HEADS = 4
B_DK = 128
B_DV = 128
B_CONV = 4
B_CHUNK = 64
C_HEADS = 4
C_DK = 128
C_DV = 128
NORM_EPS = 1e-6
MASK_VALUE = -1e30
DEEPNORM_ALPHA = (2.0 * DEPTH) ** 0.25
LOG2_E = 1.4426950408889634

IN_SPLITS = (A_Q_LORA, A_KV_LORA, A_ROPE, BRANCH_W, 3 * B_HEADS * B_DK, B_HEADS, B_HEADS, BRANCH_W,
             C_HEADS * C_DK, C_HEADS * C_DK, C_HEADS * C_DV, BRANCH_W, N_BRANCH * D_MODEL)
IN_OFFSETS = tuple(int(o) for o in np.cumsum((0,) + IN_SPLITS))

LANES = 128
SUBLANES = 8
A_HEAD_PAD = LANES
VMEM_LIMIT = 56 * 1024 * 1024

TOKEN_TILE = 512
ATTN_TILE = 512
ATTN_GROUP = 4
REC_TILE = 256
GDN_TILE = 512
HGRN_SUB = SUBLANES
HGRN_GROUP = 16
HGRN_DENSE_MIN = 32


def _sigmoid(x):
    return 1.0 / (1.0 + jnp.exp(-x))


def _silu(x):
    return x * _sigmoid(x)


def _softplus(x):
    return jnp.maximum(x, 0.0) + jnp.log(1.0 + jnp.exp(-jnp.abs(x)))


def _rms(x, g):
    return x * lax.rsqrt(jnp.mean(x * x, axis=-1, keepdims=True) + NORM_EPS) * g


def _dot(a, b):
    return jnp.dot(a, b, preferred_element_type=F32)


def _dot_nt(a, b):
    return lax.dot_general(a, b, (((1,), (1,)), ((), ())), preferred_element_type=F32)


def _dot_tn(a, b):
    return lax.dot_general(a, b, (((0,), (0,)), ((), ())), preferred_element_type=F32)


def _const_spec(shape):
    nd = len(shape)
    return pl.BlockSpec(shape, lambda *_: (0,) * nd, pipeline_mode=pl.Buffered(1))


def _layer_spec(arr, layer):
    nd = arr.ndim - 1
    return pl.BlockSpec((None,) + arr.shape[1:], lambda *_: (layer,) + (0,) * nd, pipeline_mode=pl.Buffered(1))


def _params(*sem):
    return pltpu.CompilerParams(dimension_semantics=sem, vmem_limit_bytes=VMEM_LIMIT)


def _mla_prep_kernel(xb_ref, cos_ref, sin_ref, wa_ref, qg_ref, kvg_ref, wq_ref, wqr_ref, wk_ref, wv_ref,
                     q_ref, k_ref, v_ref):
    h = _dot(xb_ref[...], wa_ref[...])
    cq = h[:, :A_Q_LORA]
    ckv = h[:, A_Q_LORA:A_Q_LORA + A_KV_LORA]
    kr = h[:, A_Q_LORA + A_KV_LORA:A_Q_LORA + A_KV_LORA + LANES]
    kr_rot = h[:, A_Q_LORA + A_KV_LORA + LANES:]
    cqn = _rms(cq, qg_ref[...]).astype(BF16)
    ckvn = _rms(ckv, kvg_ref[...]).astype(BF16)
    cos = cos_ref[...]
    sin = sin_ref[...]
    lane = lax.broadcasted_iota(jnp.int32, cos.shape, 1)
    cq_tab = jnp.where(lane < A_NOPE, 1.0, cos) * (A_SCALE * LOG2_E)
    sq_tab = sin * (A_SCALE * LOG2_E)
    cq_tab = jnp.concatenate([cq_tab] * A_HEADS, axis=1)
    sq_tab = jnp.concatenate([sq_tab] * A_HEADS, axis=1)
    q = _dot(cqn, wq_ref[...]) * cq_tab + _dot(cqn, wqr_ref[...]) * sq_tab
    q_ref[...] = q.astype(BF16)
    k_rope = kr * cos + kr_rot * sin
    k = _dot(ckvn, wk_ref[...]) + jnp.concatenate([k_rope] * A_HEADS, axis=1)
    k_ref[...] = k.astype(BF16)
    v = _dot(ckvn, wv_ref[...])
    vlane = lax.broadcasted_iota(jnp.int32, v.shape, 1) % A_HEAD_PAD
    v_ref[...] = jnp.where(vlane == A_VDIM, 1.0, v).astype(BF16)


def _mla_prep(layer, xb, cos_t, sin_t, wa, qg, kvg, wq, wqr, wk, wv):
    t = xb.shape[0]
    tm = min(TOKEN_TILE, t)
    row = lambda i: (i, 0)
    hp = A_HEADS * A_HEAD_PAD
    return pl.pallas_call(
        _mla_prep_kernel,
        grid=(t // tm,),
        in_specs=[pl.BlockSpec((tm, D_MODEL), row), pl.BlockSpec((tm, LANES), row), pl.BlockSpec((tm, LANES), row),
                  *[_layer_spec(a, layer) for a in (wa, qg, kvg, wq, wqr, wk, wv)]],
        out_specs=[pl.BlockSpec((tm, hp), row), pl.BlockSpec((tm, hp), row), pl.BlockSpec((tm, hp), row)],
        out_shape=[jax.ShapeDtypeStruct((t, hp), BF16)] * 3,
        compiler_params=_params("parallel"),
        name="mla_prep",
    )(xb, cos_t, sin_t, wa, qg, kvg, wq, wqr, wk, wv)


def _flash_kernel(q_ref, k_ref, v_ref, o_ref, *, blk):
    qi = pl.program_id(2)
    qs = (q_ref[:, :A_HEAD_PAD], q_ref[:, A_HEAD_PAD:])
    heads = tuple(slice(hh * A_HEAD_PAD, (hh + 1) * A_HEAD_PAD) for hh in range(2))
    row = lax.broadcasted_iota(jnp.int32, (blk, blk), 0)
    col = lax.broadcasted_iota(jnp.int32, (blk, blk), 1)

    def scores(j):
        kk = k_ref[pl.ds(pl.multiple_of(j * blk, blk), blk), :]
        return [_dot_nt(qs[hh], kk[:, heads[hh]]) for hh in range(2)]

    def update(carry, j, s, masked):
        vv = v_ref[pl.ds(pl.multiple_of(j * blk, blk), blk), :]
        out = []
        for hh in range(2):
            m, acc = carry[2 * hh:2 * hh + 2]
            sh = jnp.where(col <= row, s[hh], MASK_VALUE) if masked else s[hh]
            m_new = jnp.maximum(m, jnp.max(sh, axis=1, keepdims=True))
            p = jnp.exp2(sh - m_new)
            acc = jnp.exp2(m - m_new) * acc + _dot(p.astype(BF16), vv[:, heads[hh]])
            out += [m_new, acc]
        return tuple(out)

    def run(carry, first, count, diagonal):
        js = [first + t for t in range(count)]
        ss = [scores(j) for j in js]
        for t in range(count):
            carry = update(carry, js[t], ss[t], diagonal and t == count - 1)
        return carry

    carry = (jnp.full((blk, 1), -jnp.inf, F32), jnp.zeros((blk, A_HEAD_PAD), F32)) * 2
    groups = qi // ATTN_GROUP
    carry = lax.fori_loop(0, groups, lambda i, c: run(c, ATTN_GROUP * i, ATTN_GROUP, False), carry)
    first = groups * ATTN_GROUP
    tails = [functools.partial(run, first=first, count=r + 1, diagonal=True) for r in range(ATTN_GROUP)]
    _, acc0, _, acc1 = lax.switch(qi - first, tails, carry)
    o0 = acc0 / acc0[:, A_VDIM:A_VDIM + 1]
    o1 = acc1 / acc1[:, A_VDIM:A_VDIM + 1]
    lane = lax.broadcasted_iota(jnp.int32, o0.shape, 1)
    o_ref[...] = jnp.where(lane < A_VDIM, o0, pltpu.roll(o1, A_VDIM, 1))


def _mla_flash(q, k, v, batch, seq):
    blk = min(ATTN_TILE, seq)
    nq = seq // blk
    return pl.pallas_call(
        functools.partial(_flash_kernel, blk=blk),
        grid=(batch, A_HEADS // 2, nq),
        in_specs=[pl.BlockSpec((blk, 2 * A_HEAD_PAD), lambda b, h, i: (b * nq + i, h)),
                  pl.BlockSpec((seq, 2 * A_HEAD_PAD), lambda b, h, i: (b, h)),
                  pl.BlockSpec((seq, 2 * A_HEAD_PAD), lambda b, h, i: (b, h))],
        out_specs=pl.BlockSpec((blk, 2 * A_VDIM), lambda b, h, i: (b * nq + i, h)),
        out_shape=jax.ShapeDtypeStruct((batch * seq, BRANCH_W), F32),
        compiler_params=_params("parallel", "parallel", "arbitrary"),
        name="mla_flash",
    )(q, k, v)


def _cumsum_rows(x, period):
    pos = lax.broadcasted_iota(jnp.int32, x.shape, 0) % period
    s = 1
    while s < period:
        x = x + jnp.where(pos >= s, pltpu.roll(x, s, 0), 0.0)
        s *= 2
    return x


def _bdot(a, b):
    return lax.dot_general(a, b, (((2,), (1,)), ((0,), (0,))), preferred_element_type=F32)


def _bdot_nt(a, b):
    return lax.dot_general(a, b, (((2,), (2,)), ((0,), (0,))), preferred_element_type=F32)


def _unit_lower_inverse(low):
    c = low.shape[-1]
    eye = (lax.broadcasted_iota(jnp.int32, (c, c), 0) == lax.broadcasted_iota(jnp.int32, (c, c), 1)).astype(F32)
    inv = eye - low
    pw16 = low.astype(BF16)
    k = 2
    while k < c:
        pw16 = _bdot(pw16, pw16).astype(BF16)
        inv = inv + _bdot(inv.astype(BF16), pw16)
        k *= 2
    return inv


def _gdn_kernel(xb_ref, w_ref, convw_ref, alog_ref, dtb_ref, ng_ref, o_ref,
                state_ref, xbuf_ref, m_s, n_s, qp_s, o0_s, egl_s, out_s, *, tile):
    c = B_CHUNK
    n = tile // c
    nb = B_HEADS * n
    hd = B_HEADS * B_DK

    @pl.when(pl.program_id(1) == 0)
    def _():
        state_ref[...] = jnp.zeros_like(state_ref)
        xbuf_ref[0:SUBLANES, :] = jnp.zeros((SUBLANES, 3 * hd), F32)

    h = _dot(xb_ref[...], w_ref[...])
    xbuf_ref[SUBLANES:, :] = h[:, :3 * hd]
    conv = h[:, :3 * hd] * convw_ref[B_CONV - 1:B_CONV, :]
    for s in range(1, B_CONV):
        conv = conv + xbuf_ref[pl.ds(SUBLANES - s, tile), :] * convw_ref[B_CONV - 1 - s:B_CONV - s, :]
    xbuf_ref[0:SUBLANES, :] = h[tile - SUBLANES:, :3 * hd]
    qkv = _silu(conv)

    gates = h[:, 3 * hd:]
    g_all = -jnp.exp(alog_ref[...]) * _softplus(gates + dtb_ref[...])
    gcum = _cumsum_rows(g_all, c)
    beta_all = _sigmoid(gates)

    def per_head(fn):
        return jnp.concatenate([fn(hh) for hh in range(B_HEADS)], axis=0)

    def unit(x):
        return x * lax.rsqrt(jnp.sum(x * x, axis=-1, keepdims=True) + NORM_EPS)

    q = per_head(lambda hh: (unit(qkv[:, hh * B_DK:(hh + 1) * B_DK]) * (B_DK ** -0.5)).reshape(n, c, B_DK))
    k = per_head(lambda hh: unit(qkv[:, hd + hh * B_DK:hd + (hh + 1) * B_DK]).reshape(n, c, B_DK))
    v = per_head(lambda hh: qkv[:, 2 * hd + hh * B_DV:2 * hd + (hh + 1) * B_DV].reshape(n, c, B_DV))
    gc = per_head(lambda hh: gcum[:, hh:hh + 1].reshape(n, c, 1))
    beta = per_head(lambda hh: beta_all[:, B_HEADS + hh:B_HEADS + hh + 1].reshape(n, c, 1))

    ri = lax.broadcasted_iota(jnp.int32, (c, c), 0)
    ci = lax.broadcasted_iota(jnp.int32, (c, c), 1)
    tril = ci <= ri
    gcb = jnp.broadcast_to(gc, (nb, c, c))
    diff = gcb - jnp.swapaxes(gcb, 1, 2)
    decay = jnp.where(tril, jnp.exp(jnp.where(tril, diff, 0.0)), 0.0)
    k16 = k.astype(BF16)
    k_beta = k * beta
    low = jnp.where(ci < ri, _bdot_nt(k_beta.astype(BF16), k16) * decay, 0.0)
    inv = _unit_lower_inverse(low)
    eg = jnp.exp(gc)
    rhs = jnp.concatenate([v * beta, k_beta * eg], axis=2)
    uw16 = _bdot(inv.astype(BF16), rhs.astype(BF16)).astype(BF16)
    intra16 = (_bdot_nt(q.astype(BF16), k16) * decay).astype(BF16)
    g_last = gc[:, c - 1:c, :]
    kd_t = jnp.swapaxes(k * jnp.exp(g_last - gc), 1, 2).astype(BF16)
    kuw = _bdot(kd_t, uw16)
    iuw = _bdot(intra16, uw16)
    n_s[...] = kuw[:, :, :B_DV]
    m_s[...] = kuw[:, :, B_DV:].astype(BF16)
    o0_s[...] = iuw[:, :, :B_DV]
    qp_s[...] = (q * eg - iuw[:, :, B_DV:]).astype(BF16)
    egl_s[...] = jnp.broadcast_to(jnp.exp(g_last), (nb, 1, B_DV))

    heads = range(B_HEADS)
    st = [state_ref[hh] for hh in heads]
    for nn in range(n):
        st16 = [s.astype(BF16) for s in st]
        ms = [_dot(m_s[hh * n + nn], st16[hh]) for hh in heads]
        for hh in heads:
            out_s[nn * c:(nn + 1) * c, hh * B_DV:(hh + 1) * B_DV] = _dot(qp_s[hh * n + nn], st16[hh]) + o0_s[hh * n + nn]
        st = [st[hh] * egl_s[hh * n + nn] - ms[hh] + n_s[hh * n + nn] for hh in heads]
    for hh in heads:
        state_ref[hh] = st[hh]
        o_ref[:, hh * B_DV:(hh + 1) * B_DV] = _rms(out_s[:, hh * B_DV:(hh + 1) * B_DV], ng_ref[...])


def _gdn(layer, xb, w, conv_w, a_log, dt_bias, norm_g, batch, seq):
    tile = min(GDN_TILE, seq)
    nt = seq // tile
    hd = B_HEADS * B_DK
    nb = B_HEADS * (tile // B_CHUNK)
    return pl.pallas_call(
        functools.partial(_gdn_kernel, tile=tile),
        grid=(batch, nt),
        in_specs=[pl.BlockSpec((tile, D_MODEL), lambda b, i: (b * nt + i, 0)),
                  *[_layer_spec(a, layer) for a in (w, conv_w, a_log, dt_bias, norm_g)]],
        out_specs=pl.BlockSpec((tile, BRANCH_W), lambda b, i: (b * nt + i, 0)),
        out_shape=jax.ShapeDtypeStruct((batch * seq, BRANCH_W), F32),
        scratch_shapes=[pltpu.VMEM((B_HEADS, B_DK, B_DV), F32),
                        pltpu.VMEM((SUBLANES + tile, 3 * hd), F32),
                        pltpu.VMEM((nb, B_DK, B_DK), BF16),
                        pltpu.VMEM((nb, B_DK, B_DV), F32),
                        pltpu.VMEM((nb, B_CHUNK, B_DK), BF16),
                        pltpu.VMEM((nb, B_CHUNK, B_DV), F32),
                        pltpu.VMEM((nb, 1, B_DV), F32),
                        pltpu.VMEM((tile, hd), F32)],
        compiler_params=_params("arbitrary", "arbitrary"),
        name="gdn",
    )(xb, w, conv_w, a_log, dt_bias, norm_g)


def _hgrn_levels(tile):
    out = []
    lv = tile // 2
    while lv >= HGRN_SUB:
        out.append(lv)
        lv //= 2
    return tuple(out)


def _hgrn_level_map(tile):
    i = np.arange(tile)[:, None]
    j = np.arange(tile)[None, :]
    x = np.bitwise_xor(i, j)
    lev = np.where(j < i, np.floor(np.log2(np.maximum(x, 1))).astype(np.int32), -1)
    return jnp.asarray(lev, jnp.int32)


def _hgrn_kernel(xb_ref, w_ref, lb_ref, ng_ref, lev_ref, tri_ref, o_ref,
                 state_ref, q_s, k_s, v_s, cum_s, acc_s, *, tile):
    hd = C_HEADS * C_DK

    @pl.when(pl.program_id(1) == 0)
    def _():
        state_ref[...] = jnp.zeros_like(state_ref)

    h = _dot(xb_ref[...], w_ref[...])
    lev = lev_ref[...]
    sub_row = lax.broadcasted_iota(jnp.int32, (HGRN_SUB, C_DV), 0)

    lb = lb_ref[...]
    q_all = _silu(h[:, :hd]) * (C_DK ** -0.5)
    sig = _sigmoid(h[:, hd:2 * hd])
    log_f = jnp.log(jnp.maximum(lb + (1.0 - lb) * sig, 1e-30))
    k_all = (1.0 - lb) * (1.0 - sig)
    f_hi = log_f.astype(BF16)
    r1 = log_f - f_hi.astype(F32)
    f_mid = r1.astype(BF16)
    f_lo = (r1 - f_mid.astype(F32)).astype(BF16)
    parts = _dot(tri_ref[...], jnp.concatenate([f_hi, f_mid, f_lo], axis=1))
    cum_all = parts[:, :hd] + parts[:, hd:2 * hd] + parts[:, 2 * hd:]

    heads = range(C_HEADS)
    cols = [slice(hh * C_DK, (hh + 1) * C_DK) for hh in heads]
    q = [q_all[:, sl] for sl in cols]
    k = [k_all[:, sl] for sl in cols]
    cum = [cum_all[:, sl] for sl in cols]
    v16 = [h[:, 2 * hd + hh * C_DV:2 * hd + (hh + 1) * C_DV].astype(BF16) for hh in heads]
    for hh in heads:
        q_s[hh] = q[hh]
        k_s[hh] = k[hh]
        v_s[hh] = h[:, 2 * hd + hh * C_DV:2 * hd + (hh + 1) * C_DV]
        cum_s[hh] = cum[hh]

    state_t = [state_ref[hh] for hh in heads]
    for hh in heads:
        acc_s[hh] = _dot_nt((q[hh] * jnp.exp(cum[hh])).astype(BF16), state_t[hh].astype(BF16))
    for hh in heads:
        cum_last = cum[hh][tile - 1:tile, :]
        k_dec = k[hh] * jnp.exp(cum_last - cum[hh])
        state_ref[hh] = state_t[hh] * jnp.exp(cum_last) + _dot_tn(v16[hh], k_dec.astype(BF16))

    for lv in _hgrn_levels(tile):
        if lv < HGRN_DENSE_MIN:
            continue
        blocks = tile // (2 * lv)
        shape = (blocks, 2 * lv, C_DK)
        s_lv = []
        for hh in heads:
            q3, k3, c3 = q[hh].reshape(shape), k[hh].reshape(shape), cum[hh].reshape(shape)
            ref = c3[:, lv:lv + 1, :]
            qt = q3[:, lv:, :] * jnp.exp(c3[:, lv:, :] - ref)
            kt = k3[:, :lv, :] * jnp.exp(ref - c3[:, :lv, :])
            s_lv.append(_bdot_nt(qt.astype(BF16), kt.astype(BF16)).astype(BF16))
        for hh in heads:
            o_lv = _bdot(s_lv[hh], v16[hh].reshape(shape)[:, :lv, :])
            for bb in range(blocks):
                acc_s[hh, bb * 2 * lv + lv:(bb + 1) * 2 * lv, :] += o_lv[bb]
    span = 2 * HGRN_DENSE_MIN
    shape = (tile // span, span, C_DK)
    lev_s = lev[:span, :span]
    scores = [jnp.zeros((tile // span, span, span), F32) for _ in heads]
    for lv in _hgrn_levels(tile):
        if lv >= HGRN_DENSE_MIN:
            continue
        for hh in heads:
            c3 = cum[hh].reshape(shape)
            c4 = cum[hh].reshape(tile // (2 * lv), 2 * lv, C_DK)
            ref = jnp.broadcast_to(c4[:, lv:lv + 1, :], c4.shape).reshape(shape)
            qt = q[hh].reshape(shape) * jnp.exp(jnp.minimum(c3 - ref, 0.0))
            kt = k[hh].reshape(shape) * jnp.exp(jnp.minimum(ref - c3, 0.0))
            s_lv = _bdot_nt(qt.astype(BF16), kt.astype(BF16))
            scores[hh] = jnp.where(lev_s == int(np.log2(lv)), s_lv, scores[hh])
    for hh in heads:
        acc_s[hh] += _bdot(scores[hh].astype(BF16), v16[hh].reshape(shape)).reshape(tile, C_DV)

    for hh in heads:
        def group(gi, _):
            base = pl.multiple_of(gi * (HGRN_GROUP * HGRN_SUB), HGRN_GROUP * HGRN_SUB)
            blocks = [pl.ds(base + ss * HGRN_SUB, HGRN_SUB) for ss in range(HGRN_GROUP)]
            qv = [q_s[hh, r, :] for r in blocks]
            cv = [cum_s[hh, r, :] for r in blocks]
            acc = [acc_s[hh, r, :] for r in blocks]
            for j in range(HGRN_SUB):
                for ss in range(HGRN_GROUP):
                    rj = pl.ds(base + ss * HGRN_SUB + j, 1)
                    e = jnp.exp(jnp.minimum(cv[ss] - cum_s[hh, rj, :], 0.0))
                    sc = jnp.sum(qv[ss] * k_s[hh, rj, :] * e, axis=1, keepdims=True)
                    acc[ss] = acc[ss] + jnp.where(sub_row >= j, sc, 0.0) * v_s[hh, rj, :]
            for ss in range(HGRN_GROUP):
                acc_s[hh, blocks[ss], :] = acc[ss]
            return 0

        lax.fori_loop(0, tile // (HGRN_GROUP * HGRN_SUB), group, 0)
        o_ref[:, cols[hh]] = _rms(acc_s[hh], ng_ref[...])


def _hgrn(layer, xb, w, lower_bound, norm_g, batch, seq):
    tile = min(REC_TILE, seq)
    nt = seq // tile
    lev = _hgrn_level_map(tile)
    tri = jnp.asarray(np.tril(np.ones((tile, tile), np.float32)), BF16)
    return pl.pallas_call(
        functools.partial(_hgrn_kernel, tile=tile),
        grid=(batch, nt),
        in_specs=[pl.BlockSpec((tile, D_MODEL), lambda b, i: (b * nt + i, 0)),
                  *[_layer_spec(a, layer) for a in (w, lower_bound, norm_g)],
                  _const_spec(lev.shape), _const_spec(tri.shape)],
        out_specs=pl.BlockSpec((tile, BRANCH_W), lambda b, i: (b * nt + i, 0)),
        out_shape=jax.ShapeDtypeStruct((batch * seq, BRANCH_W), F32),
        scratch_shapes=[pltpu.VMEM((C_HEADS, C_DV, C_DK), F32)]
                       + [pltpu.VMEM((C_HEADS, tile, C_DK), F32) for _ in range(5)],
        compiler_params=_params("arbitrary", "arbitrary"),
        name="hgrn2",
    )(xb, w, lower_bound, norm_g, lev, tri)


def _merge_kernel(x_ref, xb_ref, oa_ref, ob_ref, oc_ref, p_ref, wz_ref, wg_ref, wbr_ref, wout_ref,
                  pgate_ref, pproj_ref, lng_ref, lnb_ref, xo_ref, xbo_ref):
    xb = xb_ref[...]
    z = _dot(xb, wz_ref[...])
    merged = None
    for n, o_ref in enumerate((oa_ref, ob_ref, oc_ref)):
        y = o_ref[...] * _silu(z[:, n * BRANCH_W:(n + 1) * BRANCH_W])
        proj = _dot(y.astype(BF16), wbr_ref[n])
        gate = _sigmoid(_dot(xb, wg_ref[:, n * D_MODEL:(n + 1) * D_MODEL]))
        merged = gate * proj if merged is None else merged + gate * proj
    r = DEEPNORM_ALPHA * x_ref[...] + _dot(merged.astype(BF16), wout_ref[...])
    ple = _dot(p_ref[...].astype(BF16), pproj_ref[...])
    r = r + _sigmoid(_dot(r.astype(BF16), pgate_ref[...])) * ple
    mu = jnp.mean(r, axis=-1, keepdims=True)
    var = jnp.mean(jnp.square(r - mu), axis=-1, keepdims=True)
    xn = (r - mu) * lax.rsqrt(var + NORM_EPS) * lng_ref[...] + lnb_ref[...]
    xo_ref[...] = xn
    xbo_ref[...] = xn.astype(BF16)


def _merge(layer, x, xb, oa, ob, oc, p, wz, wg, wbr, wout, pgate, pproj, lng, lnb):
    t = x.shape[0]
    seq = p.shape[2]
    tm = min(TOKEN_TILE, seq)
    per_seq = seq // tm
    row = lambda i: (i, 0)
    return pl.pallas_call(
        _merge_kernel,
        grid=(t // tm,),
        in_specs=[pl.BlockSpec((tm, D_MODEL), row), pl.BlockSpec((tm, D_MODEL), row),
                  pl.BlockSpec((tm, BRANCH_W), row), pl.BlockSpec((tm, BRANCH_W), row),
                  pl.BlockSpec((tm, BRANCH_W), row),
                  pl.BlockSpec((None, None, tm, PLE_DIM), lambda i: (layer, i // per_seq, i % per_seq, 0)),
                  *[_layer_spec(a, layer) for a in (wz, wg, wbr, wout, pgate, pproj, lng, lnb)]],
        out_specs=[pl.BlockSpec((tm, D_MODEL), row), pl.BlockSpec((tm, D_MODEL), row)],
        out_shape=[jax.ShapeDtypeStruct((t, D_MODEL), F32), jax.ShapeDtypeStruct((t, D_MODEL), BF16)],
        compiler_params=_params("parallel"),
        name="merge",
    )(x, xb, oa, ob, oc, p, wz, wg, wbr, wout, pgate, pproj, lng, lnb)


def _cols(w, idx):
    return w[..., IN_OFFSETS[idx]:IN_OFFSETS[idx + 1]]


def _rot_cols(w):
    half = w.shape[-1] // 2
    return jnp.concatenate([-w[..., half:], w[..., :half]], axis=-1)


def _pad_last(w, left, total):
    return jnp.pad(w, ((0, 0),) * (w.ndim - 1) + ((left, total - left - w.shape[-1]),))


def _mla_weights(w_in, w_uq, w_ukv):
    depth = w_in.shape[0]
    w_kr = _cols(w_in, 2)
    wa = jnp.concatenate([_cols(w_in, 0), _cols(w_in, 1), _pad_last(w_kr, A_NOPE, LANES),
                          _pad_last(_rot_cols(w_kr), A_NOPE, LANES)], axis=-1).astype(BF16)
    uq = w_uq.reshape(depth, A_Q_LORA, A_HEADS, A_NOPE + A_ROPE)
    wq = _pad_last(uq, 0, A_HEAD_PAD).reshape(depth, A_Q_LORA, A_HEADS * A_HEAD_PAD).astype(BF16)
    uq_rot = jnp.concatenate([jnp.zeros_like(uq[..., :A_NOPE]), _rot_cols(uq[..., A_NOPE:])], axis=-1)
    wqr = _pad_last(uq_rot, 0, A_HEAD_PAD).reshape(depth, A_Q_LORA, A_HEADS * A_HEAD_PAD).astype(BF16)
    ukv = w_ukv.reshape(depth, A_KV_LORA, A_HEADS, A_NOPE + A_VDIM)
    wk = _pad_last(ukv[..., :A_NOPE], 0, A_HEAD_PAD).reshape(depth, A_KV_LORA, A_HEADS * A_HEAD_PAD).astype(BF16)
    wv = _pad_last(ukv[..., A_NOPE:], 0, A_HEAD_PAD).reshape(depth, A_KV_LORA, A_HEADS * A_HEAD_PAD).astype(BF16)
    return wa, wq, wqr, wk, wv


def _rope_tables(positions):
    inv = ROPE_THETA ** (-jnp.arange(0, A_ROPE, 2, dtype=F32) / A_ROPE)
    ang = positions.astype(F32).reshape(-1, 1) * inv
    cos, sin = lax.optimization_barrier((jnp.cos(ang), jnp.sin(ang)))
    place = lambda t: _pad_last(jnp.concatenate([t, t], axis=-1), A_NOPE, LANES)
    return place(cos), place(sin)


def _lower_bounds(logits):
    pr = jax.nn.softmax(logits.astype(F32), axis=0)
    return jnp.clip(jnp.cumsum(pr, axis=0) - pr[0:1], 0.0, 1.0 - 1e-6)


def _rows(v):
    return v.astype(F32)[:, None, :]


def kernel(x, p, positions, w_in, a_q_norm, a_w_uq, a_kv_norm, a_w_ukv, b_conv, b_a_log, b_dt_bias, b_norm,
           c_lb_logits, c_norm, w_branch, w_out, ple_proj, ple_gate, ln_g, ln_b):
    batch, seq, _ = x.shape
    t = batch * seq
    depth = w_in.shape[0]
    cos_t, sin_t = _rope_tables(positions)

    wa, wq, wqr, wk, wv = _mla_weights(w_in, a_w_uq, a_w_ukv)
    gate_cols = jnp.concatenate([_cols(w_in, 5), _cols(w_in, 6)], axis=-1)
    wb = jnp.concatenate([_cols(w_in, 4), _pad_last(gate_cols, 0, LANES)], axis=-1).astype(BF16)
    wc = jnp.concatenate([_cols(w_in, 8), _cols(w_in, 9), _cols(w_in, 10)], axis=-1).astype(BF16)
    wz = jnp.concatenate([_cols(w_in, 3), _cols(w_in, 7), _cols(w_in, 11)], axis=-1).astype(BF16)
    wg = _cols(w_in, 12).astype(BF16)
    wbr, wout, pgate, pproj = (w.astype(BF16) for w in (w_branch, w_out, ple_gate, ple_proj))
    qg, kvg, ng_b, ng_c, lng, lnb = (_rows(v) for v in (a_q_norm, a_kv_norm, b_norm, c_norm, ln_g, ln_b))
    conv_w = b_conv.reshape(depth, B_CONV, -1)
    a_log = _rows(_pad_last(b_a_log, 0, LANES))
    dt_bias = _rows(_pad_last(b_dt_bias, 0, LANES))
    lower_bounds = _rows(_lower_bounds(c_lb_logits))

    xf = x.reshape(t, D_MODEL)
    xb = xf.astype(BF16)
    for i in range(depth):
        q, k, v = _mla_prep(i, xb, cos_t, sin_t, wa, qg, kvg, wq, wqr, wk, wv)
        o_a = _mla_flash(q, k, v, batch, seq)
        o_b = _gdn(i, xb, wb, conv_w, a_log, dt_bias, ng_b, batch, seq)
        o_c = _hgrn(i, xb, wc, lower_bounds, ng_c, batch, seq)
        xf, xb = _merge(i, xf, xb, o_a, o_b, o_c, p, wz, wg, wbr, wout, pgate, pproj, lng, lnb)
    return xf.reshape(batch, seq, D_MODEL)
```

```python
import functools

import numpy as np
import jax
import jax.numpy as jnp
from jax import lax
from jax.experimental import pallas as pl
from jax.experimental.pallas import tpu as pltpu

F32 = jnp.float32
BF16 = jnp.bfloat16

D_MODEL = 1024
DEPTH = 4
PLE_DIM = 256
BRANCH_W = 512
N_BRANCH = 3
A_HEADS = 8
A_NOPE = 64
A_ROPE = 32
A_VDIM = 64
A_Q_LORA = 256
A_KV_LORA = 128
A_SCALE = (A_NOPE + A_ROPE) ** -0.5
ROPE_THETA = 10000.0
B_HEADS = 4
B_DK = 128
B_DV = 128
B_CONV = 4
B_CHUNK = 64
C_HEADS = 4
C_DK = 128
C_DV = 128
NORM_EPS = 1e-6
MASK_VALUE = -1e30
DEEPNORM_ALPHA = (2.0 * DEPTH) ** 0.25
LOG2_E = 1.4426950408889634

IN_SPLITS = (A_Q_LORA, A_KV_LORA, A_ROPE, BRANCH_W, 3 * B_HEADS * B_DK, B_HEADS, B_HEADS, BRANCH_W,
             C_HEADS * C_DK, C_HEADS * C_DK, C_HEADS * C_DV, BRANCH_W, N_BRANCH * D_MODEL)
IN_OFFSETS = tuple(int(o) for o in np.cumsum((0,) + IN_SPLITS))

LANES = 128
SUBLANES = 8
A_HEAD_PAD = LANES
VMEM_LIMIT = 56 * 1024 * 1024

TOKEN_TILE = 512
ATTN_TILE = 512
ATTN_GROUP = 8
REC_TILE = 256
GDN_TILE = 512
HGRN_SUB = SUBLANES
HGRN_GROUP = 16
HGRN_DENSE_MIN = 32


def _sigmoid(x):
    return 1.0 / (1.0 + jnp.exp(-x))


def _silu(x):
    return x * _sigmoid(x)


def _softplus(x):
    return jnp.maximum(x, 0.0) + jnp.log(1.0 + jnp.exp(-jnp.abs(x)))


def _rms(x, g):
    return x * lax.rsqrt(jnp.mean(x * x, axis=-1, keepdims=True) + NORM_EPS) * g


def _dot(a, b):
    return jnp.dot(a, b, preferred_element_type=F32)


def _dot_nt(a, b):
    return lax.dot_general(a, b, (((1,), (1,)), ((), ())), preferred_element_type=F32)


def _dot_tn(a, b):
    return lax.dot_general(a, b, (((0,), (0,)), ((), ())), preferred_element_type=F32)


def _const_spec(shape):
    nd = len(shape)
    return pl.BlockSpec(shape, lambda *_: (0,) * nd, pipeline_mode=pl.Buffered(1))


def _layer_spec(arr, layer):
    nd = arr.ndim - 1
    return pl.BlockSpec((None,) + arr.shape[1:], lambda *_: (layer,) + (0,) * nd, pipeline_mode=pl.Buffered(1))


def _params(*sem):
    return pltpu.CompilerParams(dimension_semantics=sem, vmem_limit_bytes=VMEM_LIMIT)


def _mla_prep_kernel(xb_ref, cos_ref, sin_ref, wa_ref, qg_ref, kvg_ref, wq_ref, wqr_ref, wk_ref, wv_ref,
                     q_ref, k_ref, v_ref):
    h = _dot(xb_ref[...], wa_ref[...])
    cq = h[:, :A_Q_LORA]
    ckv = h[:, A_Q_LORA:A_Q_LORA + A_KV_LORA]
    kr = h[:, A_Q_LORA + A_KV_LORA:A_Q_LORA + A_KV_LORA + LANES]
    kr_rot = h[:, A_Q_LORA + A_KV_LORA + LANES:]
    cqn = _rms(cq, qg_ref[...]).astype(BF16)
    ckvn = _rms(ckv, kvg_ref[...]).astype(BF16)
    cos = cos_ref[...]
    sin = sin_ref[...]
    lane = lax.broadcasted_iota(jnp.int32, cos.shape, 1)
    cq_tab = jnp.where(lane < A_NOPE, 1.0, cos) * (A_SCALE * LOG2_E)
    sq_tab = sin * (A_SCALE * LOG2_E)
    cq_tab = jnp.concatenate([cq_tab] * A_HEADS, axis=1)
    sq_tab = jnp.concatenate([sq_tab] * A_HEADS, axis=1)
    q = _dot(cqn, wq_ref[...]) * cq_tab + _dot(cqn, wqr_ref[...]) * sq_tab
    q_ref[...] = q.astype(BF16)
    k_rope = kr * cos + kr_rot * sin
    k = _dot(ckvn, wk_ref[...]) + jnp.concatenate([k_rope] * A_HEADS, axis=1)
    k_ref[...] = k.astype(BF16)
    v = _dot(ckvn, wv_ref[...])
    vlane = lax.broadcasted_iota(jnp.int32, v.shape, 1) % A_HEAD_PAD
    v_ref[...] = jnp.where(vlane == A_VDIM, 1.0, v).astype(BF16)


def _mla_prep(layer, xb, cos_t, sin_t, wa, qg, kvg, wq, wqr, wk, wv):
    t = xb.shape[0]
    tm = min(TOKEN_TILE, t)
    row = lambda i: (i, 0)
    hp = A_HEADS * A_HEAD_PAD
    return pl.pallas_call(
        _mla_prep_kernel,
        grid=(t // tm,),
        in_specs=[pl.BlockSpec((tm, D_MODEL), row), pl.BlockSpec((tm, LANES), row), pl.BlockSpec((tm, LANES), row),
                  *[_layer_spec(a, layer) for a in (wa, qg, kvg, wq, wqr, wk, wv)]],
        out_specs=[pl.BlockSpec((tm, hp), row), pl.BlockSpec((tm, hp), row), pl.BlockSpec((tm, hp), row)],
        out_shape=[jax.ShapeDtypeStruct((t, hp), BF16)] * 3,
        compiler_params=_params("parallel"),
        name="mla_prep",
    )(xb, cos_t, sin_t, wa, qg, kvg, wq, wqr, wk, wv)


def _flash_kernel(q_ref, k_ref, v_ref, o_ref, *, blk):
    qi = pl.program_id(2)
    qs = (q_ref[:, :A_HEAD_PAD], q_ref[:, A_HEAD_PAD:])
    heads = tuple(slice(hh * A_HEAD_PAD, (hh + 1) * A_HEAD_PAD) for hh in range(2))
    row = lax.broadcasted_iota(jnp.int32, (blk, blk), 0)
    col = lax.broadcasted_iota(jnp.int32, (blk, blk), 1)

    def scores(j):
        kk = k_ref[pl.ds(pl.multiple_of(j * blk, blk), blk), :]
        return [_dot_nt(qs[hh], kk[:, heads[hh]]) for hh in range(2)]

    def update(carry, j, s, masked):
        vv = v_ref[pl.ds(pl.multiple_of(j * blk, blk), blk), :]
        out = []
        for hh in range(2):
            m, acc = carry[2 * hh:2 * hh + 2]
            sh = jnp.where(col <= row, s[hh], MASK_VALUE) if masked else s[hh]
            m_new = jnp.maximum(m, jnp.max(sh, axis=1, keepdims=True))
            p = jnp.exp2(sh - m_new)
            acc = jnp.exp2(m - m_new) * acc + _dot(p.astype(BF16), vv[:, heads[hh]])
            out += [m_new, acc]
        return tuple(out)

    def run(carry, first, count, diagonal):
        js = [first + t for t in range(count)]
        ss = [scores(j) for j in js]
        for t in range(count):
            carry = update(carry, js[t], ss[t], diagonal and t == count - 1)
        return carry

    carry = (jnp.full((blk, 1), -jnp.inf, F32), jnp.zeros((blk, A_HEAD_PAD), F32)) * 2
    groups = qi // ATTN_GROUP
    carry = lax.fori_loop(0, groups, lambda i, c: run(c, ATTN_GROUP * i, ATTN_GROUP, False), carry)
    first = groups * ATTN_GROUP
    tails = [functools.partial(run, first=first, count=r + 1, diagonal=True) for r in range(ATTN_GROUP)]
    _, acc0, _, acc1 = lax.switch(qi - first, tails, carry)
    o0 = acc0 / acc0[:, A_VDIM:A_VDIM + 1]
    o1 = acc1 / acc1[:, A_VDIM:A_VDIM + 1]
    lane = lax.broadcasted_iota(jnp.int32, o0.shape, 1)
    o_ref[...] = jnp.where(lane < A_VDIM, o0, pltpu.roll(o1, A_VDIM, 1))


def _mla_flash(q, k, v, batch, seq):
    blk = min(ATTN_TILE, seq)
    nq = seq // blk
    return pl.pallas_call(
        functools.partial(_flash_kernel, blk=blk),
        grid=(batch, A_HEADS // 2, nq),
        in_specs=[pl.BlockSpec((blk, 2 * A_HEAD_PAD), lambda b, h, i: (b * nq + i, h)),
                  pl.BlockSpec((seq, 2 * A_HEAD_PAD), lambda b, h, i: (b, h)),
                  pl.BlockSpec((seq, 2 * A_HEAD_PAD), lambda b, h, i: (b, h))],
        out_specs=pl.BlockSpec((blk, 2 * A_VDIM), lambda b, h, i: (b * nq + i, h)),
        out_shape=jax.ShapeDtypeStruct((batch * seq, BRANCH_W), F32),
        compiler_params=_params("parallel", "parallel", "arbitrary"),
        name="mla_flash",
    )(q, k, v)


def _cumsum_rows(x, period):
    pos = lax.broadcasted_iota(jnp.int32, x.shape, 0) % period
    s = 1
    while s < period:
        x = x + jnp.where(pos >= s, pltpu.roll(x, s, 0), 0.0)
        s *= 2
    return x


def _bdot(a, b):
    return lax.dot_general(a, b, (((2,), (1,)), ((0,), (0,))), preferred_element_type=F32)


def _bdot_nt(a, b):
    return lax.dot_general(a, b, (((2,), (2,)), ((0,), (0,))), preferred_element_type=F32)


def _unit_lower_inverse(low):
    c = low.shape[-1]
    eye = (lax.broadcasted_iota(jnp.int32, (c, c), 0) == lax.broadcasted_iota(jnp.int32, (c, c), 1)).astype(F32)
    inv = eye - low
    pw16 = low.astype(BF16)
    k = 2
    while k < c:
        pw16 = _bdot(pw16, pw16).astype(BF16)
        inv = inv + _bdot(inv.astype(BF16), pw16)
        k *= 2
    return inv


def _gdn_kernel(xb_ref, w_ref, convw_ref, alog_ref, dtb_ref, ng_ref, o_ref,
                state_ref, xbuf_ref, m_s, n_s, qp_s, o0_s, egl_s, out_s, *, tile):
    c = B_CHUNK
    n = tile // c
    nb = B_HEADS * n
    hd = B_HEADS * B_DK

    @pl.when(pl.program_id(1) == 0)
    def _():
        state_ref[...] = jnp.zeros_like(state_ref)
        xbuf_ref[0:SUBLANES, :] = jnp.zeros((SUBLANES, 3 * hd), F32)

    h = _dot(xb_ref[...], w_ref[...])
    xbuf_ref[SUBLANES:, :] = h[:, :3 * hd]
    conv = h[:, :3 * hd] * convw_ref[B_CONV - 1:B_CONV, :]
    for s in range(1, B_CONV):
        conv = conv + xbuf_ref[pl.ds(SUBLANES - s, tile), :] * convw_ref[B_CONV - 1 - s:B_CONV - s, :]
    xbuf_ref[0:SUBLANES, :] = h[tile - SUBLANES:, :3 * hd]
    qkv = _silu(conv)

    gates = h[:, 3 * hd:]
    g_all = -jnp.exp(alog_ref[...]) * _softplus(gates + dtb_ref[...])
    gcum = _cumsum_rows(g_all, c)
    beta_all = _sigmoid(gates)

    def per_head(fn):
        return jnp.concatenate([fn(hh) for hh in range(B_HEADS)], axis=0)

    def unit(x):
        return x * lax.rsqrt(jnp.sum(x * x, axis=-1, keepdims=True) + NORM_EPS)

    q = per_head(lambda hh: (unit(qkv[:, hh * B_DK:(hh + 1) * B_DK]) * (B_DK ** -0.5)).reshape(n, c, B_DK))
    k = per_head(lambda hh: unit(qkv[:, hd + hh * B_DK:hd + (hh + 1) * B_DK]).reshape(n, c, B_DK))
    v = per_head(lambda hh: qkv[:, 2 * hd + hh * B_DV:2 * hd + (hh + 1) * B_DV].reshape(n, c, B_DV))
    gc = per_head(lambda hh: gcum[:, hh:hh + 1].reshape(n, c, 1))
    beta = per_head(lambda hh: beta_all[:, B_HEADS + hh:B_HEADS + hh + 1].reshape(n, c, 1))

    ri = lax.broadcasted_iota(jnp.int32, (c, c), 0)
    ci = lax.broadcasted_iota(jnp.int32, (c, c), 1)
    tril = ci <= ri
    gcb = jnp.broadcast_to(gc, (nb, c, c))
    diff = gcb - jnp.swapaxes(gcb, 1, 2)
    decay = jnp.where(tril, jnp.exp(jnp.where(tril, diff, 0.0)), 0.0)
    k16 = k.astype(BF16)
    k_beta = k * beta
    low = jnp.where(ci < ri, _bdot_nt(k_beta.astype(BF16), k16) * decay, 0.0)
    inv = _unit_lower_inverse(low)
    eg = jnp.exp(gc)
    rhs = jnp.concatenate([v * beta, k_beta * eg], axis=2)
    uw16 = _bdot(inv.astype(BF16), rhs.astype(BF16)).astype(BF16)
    intra16 = (_bdot_nt(q.astype(BF16), k16) * decay).astype(BF16)
    g_last = gc[:, c - 1:c, :]
    kd_t = jnp.swapaxes(k * jnp.exp(g_last - gc), 1, 2).astype(BF16)
    kuw = _bdot(kd_t, uw16)
    iuw = _bdot(intra16, uw16)
    n_s[...] = kuw[:, :, :B_DV]
    m_s[...] = kuw[:, :, B_DV:].astype(BF16)
    o0_s[...] = iuw[:, :, :B_DV]
    qp_s[...] = (q * eg - iuw[:, :, B_DV:]).astype(BF16)
    egl_s[...] = jnp.broadcast_to(jnp.exp(g_last), (nb, 1, B_DV))

    heads = range(B_HEADS)
    st = [state_ref[hh] for hh in heads]
    for nn in range(n):
        st16 = [s.astype(BF16) for s in st]
        ms = [_dot(m_s[hh * n + nn], st16[hh]) for hh in heads]
        for hh in heads:
            out_s[nn * c:(nn + 1) * c, hh * B_DV:(hh + 1) * B_DV] = _dot(qp_s[hh * n + nn], st16[hh]) + o0_s[hh * n + nn]
        st = [st[hh] * egl_s[hh * n + nn] - ms[hh] + n_s[hh * n + nn] for hh in heads]
    for hh in heads:
        state_ref[hh] = st[hh]
        o_ref[:, hh * B_DV:(hh + 1) * B_DV] = _rms(out_s[:, hh * B_DV:(hh + 1) * B_DV], ng_ref[...])


def _gdn(layer, xb, w, conv_w, a_log, dt_bias, norm_g, batch, seq):
    tile = min(GDN_TILE, seq)
    nt = seq // tile
    hd = B_HEADS * B_DK
    nb = B_HEADS * (tile // B_CHUNK)
    return pl.pallas_call(
        functools.partial(_gdn_kernel, tile=tile),
        grid=(batch, nt),
        in_specs=[pl.BlockSpec((tile, D_MODEL), lambda b, i: (b * nt + i, 0)),
                  *[_layer_spec(a, layer) for a in (w, conv_w, a_log, dt_bias, norm_g)]],
        out_specs=pl.BlockSpec((tile, BRANCH_W), lambda b, i: (b * nt + i, 0)),
        out_shape=jax.ShapeDtypeStruct((batch * seq, BRANCH_W), F32),
        scratch_shapes=[pltpu.VMEM((B_HEADS, B_DK, B_DV), F32),
                        pltpu.VMEM((SUBLANES + tile, 3 * hd), F32),
                        pltpu.VMEM((nb, B_DK, B_DK), BF16),
                        pltpu.VMEM((nb, B_DK, B_DV), F32),
                        pltpu.VMEM((nb, B_CHUNK, B_DK), BF16),
                        pltpu.VMEM((nb, B_CHUNK, B_DV), F32),
                        pltpu.VMEM((nb, 1, B_DV), F32),
                        pltpu.VMEM((tile, hd), F32)],
        compiler_params=_params("arbitrary", "arbitrary"),
        name="gdn",
    )(xb, w, conv_w, a_log, dt_bias, norm_g)


def _hgrn_levels(tile):
    out = []
    lv = tile // 2
    while lv >= HGRN_SUB:
        out.append(lv)
        lv //= 2
    return tuple(out)


def _hgrn_level_map(tile):
    i = np.arange(tile)[:, None]
    j = np.arange(tile)[None, :]
    x = np.bitwise_xor(i, j)
    lev = np.where(j < i, np.floor(np.log2(np.maximum(x, 1))).astype(np.int32), -1)
    return jnp.asarray(lev, jnp.int32)


def _hgrn_kernel(xb_ref, w_ref, lb_ref, ng_ref, lev_ref, tri_ref, o_ref,
                 state_ref, q_s, k_s, v_s, cum_s, acc_s, *, tile):
    hd = C_HEADS * C_DK

    @pl.when(pl.program_id(1) == 0)
    def _():
        state_ref[...] = jnp.zeros_like(state_ref)

    h = _dot(xb_ref[...], w_ref[...])
    lev = lev_ref[...]
    sub_row = lax.broadcasted_iota(jnp.int32, (HGRN_SUB, C_DV), 0)

    lb = lb_ref[...]
    q_all = _silu(h[:, :hd]) * (C_DK ** -0.5)
    sig = _sigmoid(h[:, hd:2 * hd])
    log_f = jnp.log(jnp.maximum(lb + (1.0 - lb) * sig, 1e-30))
    k_all = (1.0 - lb) * (1.0 - sig)
    f_hi = log_f.astype(BF16)
    r1 = log_f - f_hi.astype(F32)
    f_mid = r1.astype(BF16)
    f_lo = (r1 - f_mid.astype(F32)).astype(BF16)
    parts = _dot(tri_ref[...], jnp.concatenate([f_hi, f_mid, f_lo], axis=1))
    cum_all = parts[:, :hd] + parts[:, hd:2 * hd] + parts[:, 2 * hd:]

    heads = range(C_HEADS)
    cols = [slice(hh * C_DK, (hh + 1) * C_DK) for hh in heads]
    q = [q_all[:, sl] for sl in cols]
    k = [k_all[:, sl] for sl in cols]
    cum = [cum_all[:, sl] for sl in cols]
    v16 = [h[:, 2 * hd + hh * C_DV:2 * hd + (hh + 1) * C_DV].astype(BF16) for hh in heads]
    for hh in heads:
        q_s[hh] = q[hh]
        k_s[hh] = k[hh]
        v_s[hh] = h[:, 2 * hd + hh * C_DV:2 * hd + (hh + 1) * C_DV]
        cum_s[hh] = cum[hh]

    state_t = [state_ref[hh] for hh in heads]
    for hh in heads:
        acc_s[hh] = _dot_nt((q[hh] * jnp.exp(cum[hh])).astype(BF16), state_t[hh].astype(BF16))
    for hh in heads:
        cum_last = cum[hh][tile - 1:tile, :]
        k_dec = k[hh] * jnp.exp(cum_last - cum[hh])
        state_ref[hh] = state_t[hh] * jnp.exp(cum_last) + _dot_tn(v16[hh], k_dec.astype(BF16))

    for lv in _hgrn_levels(tile):
        if lv < HGRN_DENSE_MIN:
            continue
        blocks = tile // (2 * lv)
        shape = (blocks, 2 * lv, C_DK)
        s_lv = []
        for hh in heads:
            q3, k3, c3 = q[hh].reshape(shape), k[hh].reshape(shape), cum[hh].reshape(shape)
            ref = c3[:, lv:lv + 1, :]
            qt = q3[:, lv:, :] * jnp.exp(c3[:, lv:, :] - ref)
            kt = k3[:, :lv, :] * jnp.exp(ref - c3[:, :lv, :])
            s_lv.append(_bdot_nt(qt.astype(BF16), kt.astype(BF16)).astype(BF16))
        for hh in heads:
            o_lv = _bdot(s_lv[hh], v16[hh].reshape(shape)[:, :lv, :])
            for bb in range(blocks):
                acc_s[hh, bb * 2 * lv + lv:(bb + 1) * 2 * lv, :] += o_lv[bb]
    span = 2 * HGRN_DENSE_MIN
    shape = (tile // span, span, C_DK)
    lev_s = lev[:span, :span]
    scores = [jnp.zeros((tile // span, span, span), F32) for _ in heads]
    for lv in _hgrn_levels(tile):
        if lv >= HGRN_DENSE_MIN:
            continue
        for hh in heads:
            c3 = cum[hh].reshape(shape)
            c4 = cum[hh].reshape(tile // (2 * lv), 2 * lv, C_DK)
            ref = jnp.broadcast_to(c4[:, lv:lv + 1, :], c4.shape).reshape(shape)
            qt = q[hh].reshape(shape) * jnp.exp(jnp.minimum(c3 - ref, 0.0))
            kt = k[hh].reshape(shape) * jnp.exp(jnp.minimum(ref - c3, 0.0))
            s_lv = _bdot_nt(qt.astype(BF16), kt.astype(BF16))
            scores[hh] = jnp.where(lev_s == int(np.log2(lv)), s_lv, scores[hh])
    for hh in heads:
        acc_s[hh] += _bdot(scores[hh].astype(BF16), v16[hh].reshape(shape)).reshape(tile, C_DV)

    for hh in heads:
        def group(gi, _):
            base = pl.multiple_of(gi * (HGRN_GROUP * HGRN_SUB), HGRN_GROUP * HGRN_SUB)
            blocks = [pl.ds(base + ss * HGRN_SUB, HGRN_SUB) for ss in range(HGRN_GROUP)]
            qv = [q_s[hh, r, :] for r in blocks]
            cv = [cum_s[hh, r, :] for r in blocks]
            acc = [acc_s[hh, r, :] for r in blocks]
            for j in range(HGRN_SUB):
                for ss in range(HGRN_GROUP):
                    rj = pl.ds(base + ss * HGRN_SUB + j, 1)
                    e = jnp.exp(jnp.minimum(cv[ss] - cum_s[hh, rj, :], 0.0))
                    sc = jnp.sum(qv[ss] * k_s[hh, rj, :] * e, axis=1, keepdims=True)
                    acc[ss] = acc[ss] + jnp.where(sub_row >= j, sc, 0.0) * v_s[hh, rj, :]
            for ss in range(HGRN_GROUP):
                acc_s[hh, blocks[ss], :] = acc[ss]
            return 0

        lax.fori_loop(0, tile // (HGRN_GROUP * HGRN_SUB), group, 0)
        o_ref[:, cols[hh]] = _rms(acc_s[hh], ng_ref[...])


def _hgrn(layer, xb, w, lower_bound, norm_g, batch, seq):
    tile = min(REC_TILE, seq)
    nt = seq // tile
    lev = _hgrn_level_map(tile)
    tri = jnp.asarray(np.tril(np.ones((tile, tile), np.float32)), BF16)
    return pl.pallas_call(
        functools.partial(_hgrn_kernel, tile=tile),
        grid=(batch, nt),
        in_specs=[pl.BlockSpec((tile, D_MODEL), lambda b, i: (b * nt + i, 0)),
                  *[_layer_spec(a, layer) for a in (w, lower_bound, norm_g)],
                  _const_spec(lev.shape), _const_spec(tri.shape)],
        out_specs=pl.BlockSpec((tile, BRANCH_W), lambda b, i: (b * nt + i, 0)),
        out_shape=jax.ShapeDtypeStruct((batch * seq, BRANCH_W), F32),
        scratch_shapes=[pltpu.VMEM((C_HEADS, C_DV, C_DK), F32)]
                       + [pltpu.VMEM((C_HEADS, tile, C_DK), F32) for _ in range(5)],
        compiler_params=_params("arbitrary", "arbitrary"),
        name="hgrn2",
    )(xb, w, lower_bound, norm_g, lev, tri)


def _merge_kernel(x_ref, xb_ref, oa_ref, ob_ref, oc_ref, p_ref, wz_ref, wg_ref, wbr_ref, wout_ref,
                  pgate_ref, pproj_ref, lng_ref, lnb_ref, xo_ref, xbo_ref):
    xb = xb_ref[...]
    z = _dot(xb, wz_ref[...])
    merged = None
    for n, o_ref in enumerate((oa_ref, ob_ref, oc_ref)):
        y = o_ref[...] * _silu(z[:, n * BRANCH_W:(n + 1) * BRANCH_W])
        proj = _dot(y.astype(BF16), wbr_ref[n])
        gate = _sigmoid(_dot(xb, wg_ref[:, n * D_MODEL:(n + 1) * D_MODEL]))
        merged = gate * proj if merged is None else merged + gate * proj
    r = DEEPNORM_ALPHA * x_ref[...] + _dot(merged.astype(BF16), wout_ref[...])
    ple = _dot(p_ref[...].astype(BF16), pproj_ref[...])
    r = r + _sigmoid(_dot(r.astype(BF16), pgate_ref[...])) * ple
    mu = jnp.mean(r, axis=-1, keepdims=True)
    var = jnp.mean(jnp.square(r - mu), axis=-1, keepdims=True)
    xn = (r - mu) * lax.rsqrt(var + NORM_EPS) * lng_ref[...] + lnb_ref[...]
    xo_ref[...] = xn
    xbo_ref[...] = xn.astype(BF16)


def _merge(layer, x, xb, oa, ob, oc, p, wz, wg, wbr, wout, pgate, pproj, lng, lnb):
    t = x.shape[0]
    seq = p.shape[2]
    tm = min(TOKEN_TILE, seq)
    per_seq = seq // tm
    row = lambda i: (i, 0)
    return pl.pallas_call(
        _merge_kernel,
        grid=(t // tm,),
        in_specs=[pl.BlockSpec((tm, D_MODEL), row), pl.BlockSpec((tm, D_MODEL), row),
                  pl.BlockSpec((tm, BRANCH_W), row), pl.BlockSpec((tm, BRANCH_W), row),
                  pl.BlockSpec((tm, BRANCH_W), row),
                  pl.BlockSpec((None, None, tm, PLE_DIM), lambda i: (layer, i // per_seq, i % per_seq, 0)),
                  *[_layer_spec(a, layer) for a in (wz, wg, wbr, wout, pgate, pproj, lng, lnb)]],
        out_specs=[pl.BlockSpec((tm, D_MODEL), row), pl.BlockSpec((tm, D_MODEL), row)],
        out_shape=[jax.ShapeDtypeStruct((t, D_MODEL), F32), jax.ShapeDtypeStruct((t, D_MODEL), BF16)],
        compiler_params=_params("parallel"),
        name="merge",
    )(x, xb, oa, ob, oc, p, wz, wg, wbr, wout, pgate, pproj, lng, lnb)


def _cols(w, idx):
    return w[..., IN_OFFSETS[idx]:IN_OFFSETS[idx + 1]]


def _rot_cols(w):
    half = w.shape[-1] // 2
    return jnp.concatenate([-w[..., half:], w[..., :half]], axis=-1)


def _pad_last(w, left, total):
    return jnp.pad(w, ((0, 0),) * (w.ndim - 1) + ((left, total - left - w.shape[-1]),))


def _mla_weights(w_in, w_uq, w_ukv):
    depth = w_in.shape[0]
    w_kr = _cols(w_in, 2)
    wa = jnp.concatenate([_cols(w_in, 0), _cols(w_in, 1), _pad_last(w_kr, A_NOPE, LANES),
                          _pad_last(_rot_cols(w_kr), A_NOPE, LANES)], axis=-1).astype(BF16)
    uq = w_uq.reshape(depth, A_Q_LORA, A_HEADS, A_NOPE + A_ROPE)
    wq = _pad_last(uq, 0, A_HEAD_PAD).reshape(depth, A_Q_LORA, A_HEADS * A_HEAD_PAD).astype(BF16)
    uq_rot = jnp.concatenate([jnp.zeros_like(uq[..., :A_NOPE]), _rot_cols(uq[..., A_NOPE:])], axis=-1)
    wqr = _pad_last(uq_rot, 0, A_HEAD_PAD).reshape(depth, A_Q_LORA, A_HEADS * A_HEAD_PAD).astype(BF16)
    ukv = w_ukv.reshape(depth, A_KV_LORA, A_HEADS, A_NOPE + A_VDIM)
    wk = _pad_last(ukv[..., :A_NOPE], 0, A_HEAD_PAD).reshape(depth, A_KV_LORA, A_HEADS * A_HEAD_PAD).astype(BF16)
    wv = _pad_last(ukv[..., A_NOPE:], 0, A_HEAD_PAD).reshape(depth, A_KV_LORA, A_HEADS * A_HEAD_PAD).astype(BF16)
    return wa, wq, wqr, wk, wv


def _rope_tables(positions):
    inv = ROPE_THETA ** (-jnp.arange(0, A_ROPE, 2, dtype=F32) / A_ROPE)
    ang = positions.astype(F32).reshape(-1, 1) * inv
    cos, sin = lax.optimization_barrier((jnp.cos(ang), jnp.sin(ang)))
    place = lambda t: _pad_last(jnp.concatenate([t, t], axis=-1), A_NOPE, LANES)
    return place(cos), place(sin)


def _lower_bounds(logits):
    pr = jax.nn.softmax(logits.astype(F32), axis=0)
    return jnp.clip(jnp.cumsum(pr, axis=0) - pr[0:1], 0.0, 1.0 - 1e-6)


def _rows(v):
    return v.astype(F32)[:, None, :]


def kernel(x, p, positions, w_in, a_q_norm, a_w_uq, a_kv_norm, a_w_ukv, b_conv, b_a_log, b_dt_bias, b_norm,
           c_lb_logits, c_norm, w_branch, w_out, ple_proj, ple_gate, ln_g, ln_b):
    batch, seq, _ = x.shape
    t = batch * seq
    depth = w_in.shape[0]
    cos_t, sin_t = _rope_tables(positions)

    wa, wq, wqr, wk, wv = _mla_weights(w_in, a_w_uq, a_w_ukv)
    gate_cols = jnp.concatenate([_cols(w_in, 5), _cols(w_in, 6)], axis=-1)
    wb = jnp.concatenate([_cols(w_in, 4), _pad_last(gate_cols, 0, LANES)], axis=-1).astype(BF16)
    wc = jnp.concatenate([_cols(w_in, 8), _cols(w_in, 9), _cols(w_in, 10)], axis=-1).astype(BF16)
    wz = jnp.concatenate([_cols(w_in, 3), _cols(w_in, 7), _cols(w_in, 11)], axis=-1).astype(BF16)
    wg = _cols(w_in, 12).astype(BF16)
    wbr, wout, pgate, pproj = (w.astype(BF16) for w in (w_branch, w_out, ple_gate, ple_proj))
    qg, kvg, ng_b, ng_c, lng, lnb = (_rows(v) for v in (a_q_norm, a_kv_norm, b_norm, c_norm, ln_g, ln_b))
    conv_w = b_conv.reshape(depth, B_CONV, -1)
    a_log = _rows(_pad_last(b_a_log, 0, LANES))
    dt_bias = _rows(_pad_last(b_dt_bias, 0, LANES))
    lower_bounds = _rows(_lower_bounds(c_lb_logits))

    xf = x.reshape(t, D_MODEL)
    xb = xf.astype(BF16)
    for i in range(depth):
        q, k, v = _mla_prep(i, xb, cos_t, sin_t, wa, qg, kvg, wq, wqr, wk, wv)
        o_a = _mla_flash(q, k, v, batch, seq)
        o_b = _gdn(i, xb, wb, conv_w, a_log, dt_bias, ng_b, batch, seq)
        o_c = _hgrn(i, xb, wc, lower_bounds, ng_c, batch, seq)
        xf, xb = _merge(i, xf, xb, o_a, o_b, o_c, p, wz, wg, wbr, wout, pgate, pproj, lng, lnb)
    return xf.reshape(batch, seq, D_MODEL)
```

```python
import functools

import numpy as np
import jax
import jax.numpy as jnp
from jax import lax
from jax.experimental import pallas as pl
from jax.experimental.pallas import tpu as pltpu

F32 = jnp.float32
BF16 = jnp.bfloat16

D_MODEL = 1024
DEPTH = 4
PLE_DIM = 256
BRANCH_W = 512
N_BRANCH = 3
A_HEADS = 8
A_NOPE = 64
A_ROPE = 32
A_VDIM = 64
A_Q_LORA = 256
A_KV_LORA = 128
A_SCALE = (A_NOPE + A_ROPE) ** -0.5
ROPE_THETA = 10000.0
B_HEADS = 4
B_DK = 128
B_DV = 128
B_CONV = 4
B_CHUNK = 64
C_HEADS = 4
C_DK = 128
C_DV = 128
NORM_EPS = 1e-6
MASK_VALUE = -1e30
DEEPNORM_ALPHA = (2.0 * DEPTH) ** 0.25
LOG2_E = 1.4426950408889634

IN_SPLITS = (A_Q_LORA, A_KV_LORA, A_ROPE, BRANCH_W, 3 * B_HEADS * B_DK, B_HEADS, B_HEADS, BRANCH_W,
             C_HEADS * C_DK, C_HEADS * C_DK, C_HEADS * C_DV, BRANCH_W, N_BRANCH * D_MODEL)
IN_OFFSETS = tuple(int(o) for o in np.cumsum((0,) + IN_SPLITS))

LANES = 128
SUBLANES = 8
A_HEAD_PAD = LANES
VMEM_LIMIT = 56 * 1024 * 1024

TOKEN_TILE = 512
ATTN_TILE = 512
ATTN_GROUP = 4
REC_TILE = 256
GDN_TILE = 512
HGRN_SUB = SUBLANES
HGRN_GROUP = 16
HGRN_DENSE_MIN = 32


def _sigmoid(x):
    return 1.0 / (1.0 + jnp.exp(-x))


def _silu(x):
    return x * _sigmoid(x)


def _softplus(x):
    return jnp.maximum(x, 0.0) + jnp.log(1.0 + jnp.exp(-jnp.abs(x)))


def _rms(x, g):
    return x * lax.rsqrt(jnp.mean(x * x, axis=-1, keepdims=True) + NORM_EPS) * g


def _dot(a, b):
    return jnp.dot(a, b, preferred_element_type=F32)


def _dot_nt(a, b):
    return lax.dot_general(a, b, (((1,), (1,)), ((), ())), preferred_element_type=F32)


def _dot_tn(a, b):
    return lax.dot_general(a, b, (((0,), (0,)), ((), ())), preferred_element_type=F32)


def _const_spec(shape):
    nd = len(shape)
    return pl.BlockSpec(shape, lambda *_: (0,) * nd, pipeline_mode=pl.Buffered(1))


def _layer_spec(arr, layer):
    nd = arr.ndim - 1
    return pl.BlockSpec((None,) + arr.shape[1:], lambda *_: (layer,) + (0,) * nd, pipeline_mode=pl.Buffered(1))


def _params(*sem):
    return pltpu.CompilerParams(dimension_semantics=sem, vmem_limit_bytes=VMEM_LIMIT)


def _mla_prep_kernel(xb_ref, cos_ref, sin_ref, wa_ref, qg_ref, kvg_ref, wq_ref, wqr_ref, wk_ref, wv_ref,
                     q_ref, k_ref, v_ref):
    h = _dot(xb_ref[...], wa_ref[...])
    cq = h[:, :A_Q_LORA]
    ckv = h[:, A_Q_LORA:A_Q_LORA + A_KV_LORA]
    kr = h[:, A_Q_LORA + A_KV_LORA:A_Q_LORA + A_KV_LORA + LANES]
    kr_rot = h[:, A_Q_LORA + A_KV_LORA + LANES:]
    cqn = _rms(cq, qg_ref[...]).astype(BF16)
    ckvn = _rms(ckv, kvg_ref[...]).astype(BF16)
    cos = cos_ref[...]
    sin = sin_ref[...]
    lane = lax.broadcasted_iota(jnp.int32, cos.shape, 1)
    cq_tab = jnp.where(lane < A_NOPE, 1.0, cos) * (A_SCALE * LOG2_E)
    sq_tab = sin * (A_SCALE * LOG2_E)
    cq_tab = jnp.concatenate([cq_tab] * A_HEADS, axis=1)
    sq_tab = jnp.concatenate([sq_tab] * A_HEADS, axis=1)
    q = _dot(cqn, wq_ref[...]) * cq_tab + _dot(cqn, wqr_ref[...]) * sq_tab
    q_ref[...] = q.astype(BF16)
    k_rope = kr * cos + kr_rot * sin
    k = _dot(ckvn, wk_ref[...]) + jnp.concatenate([k_rope] * A_HEADS, axis=1)
    k_ref[...] = k.astype(BF16)
    v = _dot(ckvn, wv_ref[...])
    vlane = lax.broadcasted_iota(jnp.int32, v.shape, 1) % A_HEAD_PAD
    v_ref[...] = jnp.where(vlane == A_VDIM, 1.0, v).astype(BF16)


def _mla_prep(layer, xb, cos_t, sin_t, wa, qg, kvg, wq, wqr, wk, wv):
    t = xb.shape[0]
    tm = min(TOKEN_TILE, t)
    row = lambda i: (i, 0)
    hp = A_HEADS * A_HEAD_PAD
    return pl.pallas_call(
        _mla_prep_kernel,
        grid=(t // tm,),
        in_specs=[pl.BlockSpec((tm, D_MODEL), row), pl.BlockSpec((tm, LANES), row), pl.BlockSpec((tm, LANES), row),
                  *[_layer_spec(a, layer) for a in (wa, qg, kvg, wq, wqr, wk, wv)]],
        out_specs=[pl.BlockSpec((tm, hp), row), pl.BlockSpec((tm, hp), row), pl.BlockSpec((tm, hp), row)],
        out_shape=[jax.ShapeDtypeStruct((t, hp), BF16)] * 3,
        compiler_params=_params("parallel"),
        name="mla_prep",
    )(xb, cos_t, sin_t, wa, qg, kvg, wq, wqr, wk, wv)


def _flash_kernel(q_ref, k_ref, v_ref, o_ref, *, blk):
    qi = pl.program_id(2)
    qs = (q_ref[:, :A_HEAD_PAD], q_ref[:, A_HEAD_PAD:])
    heads = tuple(slice(hh * A_HEAD_PAD, (hh + 1) * A_HEAD_PAD) for hh in range(2))
    row = lax.broadcasted_iota(jnp.int32, (blk, blk), 0)
    col = lax.broadcasted_iota(jnp.int32, (blk, blk), 1)

    def scores(j):
        kk = k_ref[pl.ds(pl.multiple_of(j * blk, blk), blk), :]
        return [_dot_nt(qs[hh], kk[:, heads[hh]]) for hh in range(2)]

    def update(carry, j, s, masked):
        vv = v_ref[pl.ds(pl.multiple_of(j * blk, blk), blk), :]
        out = []
        for hh in range(2):
            m, acc = carry[2 * hh:2 * hh + 2]
            sh = jnp.where(col <= row, s[hh], MASK_VALUE) if masked else s[hh]
            m_new = jnp.maximum(m, jnp.max(sh, axis=1, keepdims=True))
            p = jnp.exp2(sh - m_new)
            acc = jnp.exp2(m - m_new) * acc + _dot(p.astype(BF16), vv[:, heads[hh]])
            out += [m_new, acc]
        return tuple(out)

    def run(carry, first, count, diagonal):
        js = [first + t for t in range(count)]
        ss = [scores(j) for j in js]
        for t in range(count):
            carry = update(carry, js[t], ss[t], diagonal and t == count - 1)
        return carry

    carry = (jnp.full((blk, 1), -jnp.inf, F32), jnp.zeros((blk, A_HEAD_PAD), F32)) * 2
    groups = qi // ATTN_GROUP
    carry = lax.fori_loop(0, groups, lambda i, c: run(c, ATTN_GROUP * i, ATTN_GROUP, False), carry)
    first = groups * ATTN_GROUP
    tails = [functools.partial(run, first=first, count=r + 1, diagonal=True) for r in range(ATTN_GROUP)]
    _, acc0, _, acc1 = lax.switch(qi - first, tails, carry)
    o0 = acc0 / acc0[:, A_VDIM:A_VDIM + 1]
    o1 = acc1 / acc1[:, A_VDIM:A_VDIM + 1]
    lane = lax.broadcasted_iota(jnp.int32, o0.shape, 1)
    o_ref[...] = jnp.where(lane < A_VDIM, o0, pltpu.roll(o1, A_VDIM, 1)).astype(o_ref.dtype)


def _mla_flash(q, k, v, batch, seq):
    blk = min(ATTN_TILE, seq)
    nq = seq // blk
    return pl.pallas_call(
        functools.partial(_flash_kernel, blk=blk),
        grid=(batch, A_HEADS // 2, nq),
        in_specs=[pl.BlockSpec((blk, 2 * A_HEAD_PAD), lambda b, h, i: (b * nq + i, h)),
                  pl.BlockSpec((seq, 2 * A_HEAD_PAD), lambda b, h, i: (b, h)),
                  pl.BlockSpec((seq, 2 * A_HEAD_PAD), lambda b, h, i: (b, h))],
        out_specs=pl.BlockSpec((blk, 2 * A_VDIM), lambda b, h, i: (b * nq + i, h)),
        out_shape=jax.ShapeDtypeStruct((batch * seq, BRANCH_W), BF16),
        compiler_params=_params("parallel", "parallel", "arbitrary"),
        name="mla_flash",
    )(q, k, v)


def _cumsum_rows(x, period):
    pos = lax.broadcasted_iota(jnp.int32, x.shape, 0) % period
    s = 1
    while s < period:
        x = x + jnp.where(pos >= s, pltpu.roll(x, s, 0), 0.0)
        s *= 2
    return x


def _bdot(a, b):
    return lax.dot_general(a, b, (((2,), (1,)), ((0,), (0,))), preferred_element_type=F32)


def _bdot_nt(a, b):
    return lax.dot_general(a, b, (((2,), (2,)), ((0,), (0,))), preferred_element_type=F32)


def _unit_lower_inverse(low):
    c = low.shape[-1]
    eye = (lax.broadcasted_iota(jnp.int32, (c, c), 0) == lax.broadcasted_iota(jnp.int32, (c, c), 1)).astype(F32)
    inv = eye - low
    pw16 = low.astype(BF16)
    k = 2
    while k < c:
        pw16 = _bdot(pw16, pw16).astype(BF16)
        inv = inv + _bdot(inv.astype(BF16), pw16)
        k *= 2
    return inv


def _gdn_kernel(xb_ref, w_ref, convw_ref, alog_ref, dtb_ref, ng_ref, o_ref,
                state_ref, xbuf_ref, m_s, n_s, qp_s, o0_s, egl_s, out_s, *, tile):
    c = B_CHUNK
    n = tile // c
    nb = B_HEADS * n
    hd = B_HEADS * B_DK

    @pl.when(pl.program_id(1) == 0)
    def _():
        state_ref[...] = jnp.zeros_like(state_ref)
        xbuf_ref[0:SUBLANES, :] = jnp.zeros((SUBLANES, 3 * hd), F32)

    h = _dot(xb_ref[...], w_ref[...])
    xbuf_ref[SUBLANES:, :] = h[:, :3 * hd]
    conv = h[:, :3 * hd] * convw_ref[B_CONV - 1:B_CONV, :]
    for s in range(1, B_CONV):
        conv = conv + xbuf_ref[pl.ds(SUBLANES - s, tile), :] * convw_ref[B_CONV - 1 - s:B_CONV - s, :]
    xbuf_ref[0:SUBLANES, :] = h[tile - SUBLANES:, :3 * hd]
    qkv = _silu(conv)

    gates = h[:, 3 * hd:]
    g_all = -jnp.exp(alog_ref[...]) * _softplus(gates + dtb_ref[...])
    gcum = _cumsum_rows(g_all, c)
    beta_all = _sigmoid(gates)

    def per_head(fn):
        return jnp.concatenate([fn(hh) for hh in range(B_HEADS)], axis=0)

    def unit(x):
        return x * lax.rsqrt(jnp.sum(x * x, axis=-1, keepdims=True) + NORM_EPS)

    q = per_head(lambda hh: (unit(qkv[:, hh * B_DK:(hh + 1) * B_DK]) * (B_DK ** -0.5)).reshape(n, c, B_DK))
    k = per_head(lambda hh: unit(qkv[:, hd + hh * B_DK:hd + (hh + 1) * B_DK]).reshape(n, c, B_DK))
    v = per_head(lambda hh: qkv[:, 2 * hd + hh * B_DV:2 * hd + (hh + 1) * B_DV].reshape(n, c, B_DV))
    gc = per_head(lambda hh: gcum[:, hh:hh + 1].reshape(n, c, 1))
    beta = per_head(lambda hh: beta_all[:, B_HEADS + hh:B_HEADS + hh + 1].reshape(n, c, 1))

    ri = lax.broadcasted_iota(jnp.int32, (c, c), 0)
    ci = lax.broadcasted_iota(jnp.int32, (c, c), 1)
    tril = ci <= ri
    gcb = jnp.broadcast_to(gc, (nb, c, c))
    diff = gcb - jnp.swapaxes(gcb, 1, 2)
    decay = jnp.where(tril, jnp.exp(jnp.where(tril, diff, 0.0)), 0.0)
    k16 = k.astype(BF16)
    k_beta = k * beta
    low = jnp.where(ci < ri, _bdot_nt(k_beta.astype(BF16), k16) * decay, 0.0)
    inv = _unit_lower_inverse(low)
    eg = jnp.exp(gc)
    rhs = jnp.concatenate([v * beta, k_beta * eg], axis=2)
    uw16 = _bdot(inv.astype(BF16), rhs.astype(BF16)).astype(BF16)
    intra16 = (_bdot_nt(q.astype(BF16), k16) * decay).astype(BF16)
    g_last = gc[:, c - 1:c, :]
    kd_t = jnp.swapaxes(k * jnp.exp(g_last - gc), 1, 2).astype(BF16)
    kuw = _bdot(kd_t, uw16)
    iuw = _bdot(intra16, uw16)
    n_s[...] = kuw[:, :, :B_DV]
    m_s[...] = kuw[:, :, B_DV:].astype(BF16)
    o0_s[...] = iuw[:, :, :B_DV]
    qp_s[...] = (q * eg - iuw[:, :, B_DV:]).astype(BF16)
    egl_s[...] = jnp.broadcast_to(jnp.exp(g_last), (nb, 1, B_DV))

    heads = range(B_HEADS)
    st = [state_ref[hh] for hh in heads]
    for nn in range(n):
        st16 = [s.astype(BF16) for s in st]
        ms = [_dot(m_s[hh * n + nn], st16[hh]) for hh in heads]
        for hh in heads:
            out_s[nn * c:(nn + 1) * c, hh * B_DV:(hh + 1) * B_DV] = _dot(qp_s[hh * n + nn], st16[hh]) + o0_s[hh * n + nn]
        st = [st[hh] * egl_s[hh * n + nn] - ms[hh] + n_s[hh * n + nn] for hh in heads]
    for hh in heads:
        state_ref[hh] = st[hh]
        o_ref[:, hh * B_DV:(hh + 1) * B_DV] = _rms(out_s[:, hh * B_DV:(hh + 1) * B_DV], ng_ref[...]).astype(o_ref.dtype)


def _gdn(layer, xb, w, conv_w, a_log, dt_bias, norm_g, batch, seq):
    tile = min(GDN_TILE, seq)
    nt = seq // tile
    hd = B_HEADS * B_DK
    nb = B_HEADS * (tile // B_CHUNK)
    return pl.pallas_call(
        functools.partial(_gdn_kernel, tile=tile),
        grid=(batch, nt),
        in_specs=[pl.BlockSpec((tile, D_MODEL), lambda b, i: (b * nt + i, 0)),
                  *[_layer_spec(a, layer) for a in (w, conv_w, a_log, dt_bias, norm_g)]],
        out_specs=pl.BlockSpec((tile, BRANCH_W), lambda b, i: (b * nt + i, 0)),
        out_shape=jax.ShapeDtypeStruct((batch * seq, BRANCH_W), BF16),
        scratch_shapes=[pltpu.VMEM((B_HEADS, B_DK, B_DV), F32),
                        pltpu.VMEM((SUBLANES + tile, 3 * hd), F32),
                        pltpu.VMEM((nb, B_DK, B_DK), BF16),
                        pltpu.VMEM((nb, B_DK, B_DV), F32),
                        pltpu.VMEM((nb, B_CHUNK, B_DK), BF16),
                        pltpu.VMEM((nb, B_CHUNK, B_DV), F32),
                        pltpu.VMEM((nb, 1, B_DV), F32),
                        pltpu.VMEM((tile, hd), F32)],
        compiler_params=_params("arbitrary", "arbitrary"),
        name="gdn",
    )(xb, w, conv_w, a_log, dt_bias, norm_g)


def _hgrn_levels(tile):
    out = []
    lv = tile // 2
    while lv >= HGRN_SUB:
        out.append(lv)
        lv //= 2
    return tuple(out)


def _hgrn_level_map(tile):
    i = np.arange(tile)[:, None]
    j = np.arange(tile)[None, :]
    x = np.bitwise_xor(i, j)
    lev = np.where(j < i, np.floor(np.log2(np.maximum(x, 1))).astype(np.int32), -1)
    return jnp.asarray(lev, jnp.int32)


def _hgrn_kernel(xb_ref, w_ref, lb_ref, ng_ref, lev_ref, tri_ref, o_ref,
                 state_ref, q_s, k_s, v_s, cum_s, acc_s, *, tile):
    hd = C_HEADS * C_DK

    @pl.when(pl.program_id(1) == 0)
    def _():
        state_ref[...] = jnp.zeros_like(state_ref)

    h = _dot(xb_ref[...], w_ref[...])
    lev = lev_ref[...]
    sub_row = lax.broadcasted_iota(jnp.int32, (HGRN_SUB, C_DV), 0)

    lb = lb_ref[...]
    q_all = _silu(h[:, :hd]) * (C_DK ** -0.5)
    sig = _sigmoid(h[:, hd:2 * hd])
    log_f = jnp.log(jnp.maximum(lb + (1.0 - lb) * sig, 1e-30))
    k_all = (1.0 - lb) * (1.0 - sig)
    f_hi = log_f.astype(BF16)
    r1 = log_f - f_hi.astype(F32)
    f_mid = r1.astype(BF16)
    f_lo = (r1 - f_mid.astype(F32)).astype(BF16)
    parts = _dot(tri_ref[...], jnp.concatenate([f_hi, f_mid, f_lo], axis=1))
    cum_all = parts[:, :hd] + parts[:, hd:2 * hd] + parts[:, 2 * hd:]

    heads = range(C_HEADS)
    cols = [slice(hh * C_DK, (hh + 1) * C_DK) for hh in heads]
    q = [q_all[:, sl] for sl in cols]
    k = [k_all[:, sl] for sl in cols]
    cum = [cum_all[:, sl] for sl in cols]
    v16 = [h[:, 2 * hd + hh * C_DV:2 * hd + (hh + 1) * C_DV].astype(BF16) for hh in heads]
    for hh in heads:
        q_s[hh] = q[hh]
        k_s[hh] = k[hh]
        v_s[hh] = h[:, 2 * hd + hh * C_DV:2 * hd + (hh + 1) * C_DV]
        cum_s[hh] = cum[hh]

    state_t = [state_ref[hh] for hh in heads]
    for hh in heads:
        acc_s[hh] = _dot_nt((q[hh] * jnp.exp(cum[hh])).astype(BF16), state_t[hh].astype(BF16))
    for hh in heads:
        cum_last = cum[hh][tile - 1:tile, :]
        k_dec = k[hh] * jnp.exp(cum_last - cum[hh])
        state_ref[hh] = state_t[hh] * jnp.exp(cum_last) + _dot_tn(v16[hh], k_dec.astype(BF16))

    for lv in _hgrn_levels(tile):
        if lv < HGRN_DENSE_MIN:
            continue
        blocks = tile // (2 * lv)
        shape = (blocks, 2 * lv, C_DK)
        s_lv = []
        for hh in heads:
            q3, k3, c3 = q[hh].reshape(shape), k[hh].reshape(shape), cum[hh].reshape(shape)
            ref = c3[:, lv:lv + 1, :]
            qt = q3[:, lv:, :] * jnp.exp(c3[:, lv:, :] - ref)
            kt = k3[:, :lv, :] * jnp.exp(ref - c3[:, :lv, :])
            s_lv.append(_bdot_nt(qt.astype(BF16), kt.astype(BF16)).astype(BF16))
        for hh in heads:
            o_lv = _bdot(s_lv[hh], v16[hh].reshape(shape)[:, :lv, :])
            for bb in range(blocks):
                acc_s[hh, bb * 2 * lv + lv:(bb + 1) * 2 * lv, :] += o_lv[bb]
    span = 2 * HGRN_DENSE_MIN
    shape = (tile // span, span, C_DK)
    lev_s = lev[:span, :span]
    scores = [jnp.zeros((tile // span, span, span), F32) for _ in heads]
    for lv in _hgrn_levels(tile):
        if lv >= HGRN_DENSE_MIN:
            continue
        for hh in heads:
            c3 = cum[hh].reshape(shape)
            c4 = cum[hh].reshape(tile // (2 * lv), 2 * lv, C_DK)
            ref = jnp.broadcast_to(c4[:, lv:lv + 1, :], c4.shape).reshape(shape)
            qt = q[hh].reshape(shape) * jnp.exp(jnp.minimum(c3 - ref, 0.0))
            kt = k[hh].reshape(shape) * jnp.exp(jnp.minimum(ref - c3, 0.0))
            s_lv = _bdot_nt(qt.astype(BF16), kt.astype(BF16))
            scores[hh] = jnp.where(lev_s == int(np.log2(lv)), s_lv, scores[hh])
    for hh in heads:
        acc_s[hh] += _bdot(scores[hh].astype(BF16), v16[hh].reshape(shape)).reshape(tile, C_DV)

    for hh in heads:
        def group(gi, _):
            base = pl.multiple_of(gi * (HGRN_GROUP * HGRN_SUB), HGRN_GROUP * HGRN_SUB)
            blocks = [pl.ds(base + ss * HGRN_SUB, HGRN_SUB) for ss in range(HGRN_GROUP)]
            qv = [q_s[hh, r, :] for r in blocks]
            cv = [cum_s[hh, r, :] for r in blocks]
            acc = [acc_s[hh, r, :] for r in blocks]
            for j in range(HGRN_SUB):
                for ss in range(HGRN_GROUP):
                    rj = pl.ds(base + ss * HGRN_SUB + j, 1)
                    e = jnp.exp(jnp.minimum(cv[ss] - cum_s[hh, rj, :], 0.0))
                    sc = jnp.sum(qv[ss] * k_s[hh, rj, :] * e, axis=1, keepdims=True)
                    acc[ss] = acc[ss] + jnp.where(sub_row >= j, sc, 0.0) * v_s[hh, rj, :]
            for ss in range(HGRN_GROUP):
                acc_s[hh, blocks[ss], :] = acc[ss]
            return 0

        lax.fori_loop(0, tile // (HGRN_GROUP * HGRN_SUB), group, 0)
        o_ref[:, cols[hh]] = _rms(acc_s[hh], ng_ref[...]).astype(o_ref.dtype)


def _hgrn(layer, xb, w, lower_bound, norm_g, batch, seq):
    tile = min(REC_TILE, seq)
    nt = seq // tile
    lev = _hgrn_level_map(tile)
    tri = jnp.asarray(np.tril(np.ones((tile, tile), np.float32)), BF16)
    return pl.pallas_call(
        functools.partial(_hgrn_kernel, tile=tile),
        grid=(batch, nt),
        in_specs=[pl.BlockSpec((tile, D_MODEL), lambda b, i: (b * nt + i, 0)),
                  *[_layer_spec(a, layer) for a in (w, lower_bound, norm_g)],
                  _const_spec(lev.shape), _const_spec(tri.shape)],
        out_specs=pl.BlockSpec((tile, BRANCH_W), lambda b, i: (b * nt + i, 0)),
        out_shape=jax.ShapeDtypeStruct((batch * seq, BRANCH_W), BF16),
        scratch_shapes=[pltpu.VMEM((C_HEADS, C_DV, C_DK), F32)]
                       + [pltpu.VMEM((C_HEADS, tile, C_DK), F32) for _ in range(5)],
        compiler_params=_params("arbitrary", "arbitrary"),
        name="hgrn2",
    )(xb, w, lower_bound, norm_g, lev, tri)


def _merge_kernel(x_ref, xb_ref, oa_ref, ob_ref, oc_ref, p_ref, wz_ref, wg_ref, wbr_ref, wout_ref,
                  pgate_ref, pproj_ref, lng_ref, lnb_ref, xo_ref, xbo_ref):
    xb = xb_ref[...]
    z = _dot(xb, wz_ref[...])
    merged = None
    for n, o_ref in enumerate((oa_ref, ob_ref, oc_ref)):
        y = o_ref[...] * _silu(z[:, n * BRANCH_W:(n + 1) * BRANCH_W])
        proj = _dot(y.astype(BF16), wbr_ref[n])
        gate = _sigmoid(_dot(xb, wg_ref[:, n * D_MODEL:(n + 1) * D_MODEL]))
        merged = gate * proj if merged is None else merged + gate * proj
    r = DEEPNORM_ALPHA * x_ref[...] + _dot(merged.astype(BF16), wout_ref[...])
    ple = _dot(p_ref[...].astype(BF16), pproj_ref[...])
    r = r + _sigmoid(_dot(r.astype(BF16), pgate_ref[...])) * ple
    mu = jnp.mean(r, axis=-1, keepdims=True)
    var = jnp.mean(jnp.square(r - mu), axis=-1, keepdims=True)
    xn = (r - mu) * lax.rsqrt(var + NORM_EPS) * lng_ref[...] + lnb_ref[...]
    xo_ref[...] = xn
    xbo_ref[...] = xn.astype(BF16)


def _merge(layer, x, xb, oa, ob, oc, p, wz, wg, wbr, wout, pgate, pproj, lng, lnb):
    t = x.shape[0]
    seq = p.shape[2]
    tm = min(TOKEN_TILE, seq)
    per_seq = seq // tm
    row = lambda i: (i, 0)
    return pl.pallas_call(
        _merge_kernel,
        grid=(t // tm,),
        in_specs=[pl.BlockSpec((tm, D_MODEL), row), pl.BlockSpec((tm, D_MODEL), row),
                  pl.BlockSpec((tm, BRANCH_W), row), pl.BlockSpec((tm, BRANCH_W), row),
                  pl.BlockSpec((tm, BRANCH_W), row),
                  pl.BlockSpec((None, None, tm, PLE_DIM), lambda i: (layer, i // per_seq, i % per_seq, 0)),
                  *[_layer_spec(a, layer) for a in (wz, wg, wbr, wout, pgate, pproj, lng, lnb)]],
        out_specs=[pl.BlockSpec((tm, D_MODEL), row), pl.BlockSpec((tm, D_MODEL), row)],
        out_shape=[jax.ShapeDtypeStruct((t, D_MODEL), F32), jax.ShapeDtypeStruct((t, D_MODEL), BF16)],
        compiler_params=_params("parallel"),
        name="merge",
    )(x, xb, oa, ob, oc, p, wz, wg, wbr, wout, pgate, pproj, lng, lnb)


def _cols(w, idx):
    return w[..., IN_OFFSETS[idx]:IN_OFFSETS[idx + 1]]


def _rot_cols(w):
    half = w.shape[-1] // 2
    return jnp.concatenate([-w[..., half:], w[..., :half]], axis=-1)


def _pad_last(w, left, total):
    return jnp.pad(w, ((0, 0),) * (w.ndim - 1) + ((left, total - left - w.shape[-1]),))


def _mla_weights(w_in, w_uq, w_ukv):
    depth = w_in.shape[0]
    w_kr = _cols(w_in, 2)
    wa = jnp.concatenate([_cols(w_in, 0), _cols(w_in, 1), _pad_last(w_kr, A_NOPE, LANES),
                          _pad_last(_rot_cols(w_kr), A_NOPE, LANES)], axis=-1).astype(BF16)
    uq = w_uq.reshape(depth, A_Q_LORA, A_HEADS, A_NOPE + A_ROPE)
    wq = _pad_last(uq, 0, A_HEAD_PAD).reshape(depth, A_Q_LORA, A_HEADS * A_HEAD_PAD).astype(BF16)
    uq_rot = jnp.concatenate([jnp.zeros_like(uq[..., :A_NOPE]), _rot_cols(uq[..., A_NOPE:])], axis=-1)
    wqr = _pad_last(uq_rot, 0, A_HEAD_PAD).reshape(depth, A_Q_LORA, A_HEADS * A_HEAD_PAD).astype(BF16)
    ukv = w_ukv.reshape(depth, A_KV_LORA, A_HEADS, A_NOPE + A_VDIM)
    wk = _pad_last(ukv[..., :A_NOPE], 0, A_HEAD_PAD).reshape(depth, A_KV_LORA, A_HEADS * A_HEAD_PAD).astype(BF16)
    wv = _pad_last(ukv[..., A_NOPE:], 0, A_HEAD_PAD).reshape(depth, A_KV_LORA, A_HEADS * A_HEAD_PAD).astype(BF16)
    return wa, wq, wqr, wk, wv


def _rope_tables(positions):
    inv = ROPE_THETA ** (-jnp.arange(0, A_ROPE, 2, dtype=F32) / A_ROPE)
    ang = positions.astype(F32).reshape(-1, 1) * inv
    cos, sin = lax.optimization_barrier((jnp.cos(ang), jnp.sin(ang)))
    place = lambda t: _pad_last(jnp.concatenate([t, t], axis=-1), A_NOPE, LANES)
    return place(cos), place(sin)


def _lower_bounds(logits):
    pr = jax.nn.softmax(logits.astype(F32), axis=0)
    return jnp.clip(jnp.cumsum(pr, axis=0) - pr[0:1], 0.0, 1.0 - 1e-6)


def _rows(v):
    return v.astype(F32)[:, None, :]


def kernel(x, p, positions, w_in, a_q_norm, a_w_uq, a_kv_norm, a_w_ukv, b_conv, b_a_log, b_dt_bias, b_norm,
           c_lb_logits, c_norm, w_branch, w_out, ple_proj, ple_gate, ln_g, ln_b):
    batch, seq, _ = x.shape
    t = batch * seq
    depth = w_in.shape[0]
    cos_t, sin_t = _rope_tables(positions)

    wa, wq, wqr, wk, wv = _mla_weights(w_in, a_w_uq, a_w_ukv)
    gate_cols = jnp.concatenate([_cols(w_in, 5), _cols(w_in, 6)], axis=-1)
    wb = jnp.concatenate([_cols(w_in, 4), _pad_last(gate_cols, 0, LANES)], axis=-1).astype(BF16)
    wc = jnp.concatenate([_cols(w_in, 8), _cols(w_in, 9), _cols(w_in, 10)], axis=-1).astype(BF16)
    wz = jnp.concatenate([_cols(w_in, 3), _cols(w_in, 7), _cols(w_in, 11)], axis=-1).astype(BF16)
    wg = _cols(w_in, 12).astype(BF16)
    wbr, wout, pgate, pproj = (w.astype(BF16) for w in (w_branch, w_out, ple_gate, ple_proj))
    qg, kvg, ng_b, ng_c, lng, lnb = (_rows(v) for v in (a_q_norm, a_kv_norm, b_norm, c_norm, ln_g, ln_b))
    conv_w = b_conv.reshape(depth, B_CONV, -1)
    a_log = _rows(_pad_last(b_a_log, 0, LANES))
    dt_bias = _rows(_pad_last(b_dt_bias, 0, LANES))
    lower_bounds = _rows(_lower_bounds(c_lb_logits))

    xf = x.reshape(t, D_MODEL)
    xb = xf.astype(BF16)
    for i in range(depth):
        q, k, v = _mla_prep(i, xb, cos_t, sin_t, wa, qg, kvg, wq, wqr, wk, wv)
        o_a = _mla_flash(q, k, v, batch, seq)
        o_b = _gdn(i, xb, wb, conv_w, a_log, dt_bias, ng_b, batch, seq)
        o_c = _hgrn(i, xb, wc, lower_bounds, ng_c, batch, seq)
        xf, xb = _merge(i, xf, xb, o_a, o_b, o_c, p, wz, wg, wbr, wout, pgate, pproj, lng, lnb)
    return xf.reshape(batch, seq, D_MODEL)
```

```python
import functools

import numpy as np
import jax
import jax.numpy as jnp
from jax import lax
from jax.experimental import pallas as pl
from jax.experimental.pallas import tpu as pltpu

F32 = jnp.float32
BF16 = jnp.bfloat16

D_MODEL = 1024
DEPTH = 4
PLE_DIM = 256
BRANCH_W = 512
N_BRANCH = 3
A_HEADS = 8
A_NOPE = 64
A_ROPE = 32
A_VDIM = 64
A_Q_LORA = 256
A_KV_LORA = 128
A_SCALE = (A_NOPE + A_ROPE) ** -0.5
ROPE_THETA = 10000.0
B_HEADS = 4
B_DK = 128
B_DV = 128
B_CONV = 4
B_CHUNK = 64
C_HEADS = 4
C_DK = 128
C_DV = 128
NORM_EPS = 1e-6
MASK_VALUE = -1e30
DEEPNORM_ALPHA = (2.0 * DEPTH) ** 0.25
LOG2_E = 1.4426950408889634

IN_SPLITS = (A_Q_LORA, A_KV_LORA, A_ROPE, BRANCH_W, 3 * B_HEADS * B_DK, B_HEADS, B_HEADS, BRANCH_W,
             C_HEADS * C_DK, C_HEADS * C_DK, C_HEADS * C_DV, BRANCH_W, N_BRANCH * D_MODEL)
IN_OFFSETS = tuple(int(o) for o in np.cumsum((0,) + IN_SPLITS))

LANES = 128
SUBLANES = 8
A_HEAD_PAD = LANES
VMEM_LIMIT = 56 * 1024 * 1024

TOKEN_TILE = 512
ATTN_TILE = 512
ATTN_GROUP = 4
REC_TILE = 256
GDN_TILE = 512
HGRN_SUB = SUBLANES
HGRN_GROUP = 32
HGRN_DENSE_MIN = 32


def _sigmoid(x):
    return 1.0 / (1.0 + jnp.exp(-x))


def _silu(x):
    return x * _sigmoid(x)


def _softplus(x):
    return jnp.maximum(x, 0.0) + jnp.log(1.0 + jnp.exp(-jnp.abs(x)))


def _rms(x, g):
    return x * lax.rsqrt(jnp.mean(x * x, axis=-1, keepdims=True) + NORM_EPS) * g


def _dot(a, b):
    return jnp.dot(a, b, preferred_element_type=F32)


def _dot_nt(a, b):
    return lax.dot_general(a, b, (((1,), (1,)), ((), ())), preferred_element_type=F32)


def _dot_tn(a, b):
    return lax.dot_general(a, b, (((0,), (0,)), ((), ())), preferred_element_type=F32)


def _const_spec(shape):
    nd = len(shape)
    return pl.BlockSpec(shape, lambda *_: (0,) * nd, pipeline_mode=pl.Buffered(1))


def _layer_spec(arr, layer):
    nd = arr.ndim - 1
    return pl.BlockSpec((None,) + arr.shape[1:], lambda *_: (layer,) + (0,) * nd, pipeline_mode=pl.Buffered(1))


def _params(*sem):
    return pltpu.CompilerParams(dimension_semantics=sem, vmem_limit_bytes=VMEM_LIMIT)


def _mla_prep_kernel(xb_ref, cos_ref, sin_ref, wa_ref, qg_ref, kvg_ref, wq_ref, wqr_ref, wk_ref, wv_ref,
                     q_ref, k_ref, v_ref):
    h = _dot(xb_ref[...], wa_ref[...])
    cq = h[:, :A_Q_LORA]
    ckv = h[:, A_Q_LORA:A_Q_LORA + A_KV_LORA]
    kr = h[:, A_Q_LORA + A_KV_LORA:A_Q_LORA + A_KV_LORA + LANES]
    kr_rot = h[:, A_Q_LORA + A_KV_LORA + LANES:]
    cqn = _rms(cq, qg_ref[...]).astype(BF16)
    ckvn = _rms(ckv, kvg_ref[...]).astype(BF16)
    cos = cos_ref[...]
    sin = sin_ref[...]
    lane = lax.broadcasted_iota(jnp.int32, cos.shape, 1)
    cq_tab = jnp.where(lane < A_NOPE, 1.0, cos) * (A_SCALE * LOG2_E)
    sq_tab = sin * (A_SCALE * LOG2_E)
    cq_tab = jnp.concatenate([cq_tab] * A_HEADS, axis=1)
    sq_tab = jnp.concatenate([sq_tab] * A_HEADS, axis=1)
    q = _dot(cqn, wq_ref[...]) * cq_tab + _dot(cqn, wqr_ref[...]) * sq_tab
    q_ref[...] = q.astype(BF16)
    k_rope = kr * cos + kr_rot * sin
    k = _dot(ckvn, wk_ref[...]) + jnp.concatenate([k_rope] * A_HEADS, axis=1)
    k_ref[...] = k.astype(BF16)
    v = _dot(ckvn, wv_ref[...])
    vlane = lax.broadcasted_iota(jnp.int32, v.shape, 1) % A_HEAD_PAD
    v_ref[...] = jnp.where(vlane == A_VDIM, 1.0, v).astype(BF16)


def _mla_prep(layer, xb, cos_t, sin_t, wa, qg, kvg, wq, wqr, wk, wv):
    t = xb.shape[0]
    tm = min(TOKEN_TILE, t)
    row = lambda i: (i, 0)
    hp = A_HEADS * A_HEAD_PAD
    return pl.pallas_call(
        _mla_prep_kernel,
        grid=(t // tm,),
        in_specs=[pl.BlockSpec((tm, D_MODEL), row), pl.BlockSpec((tm, LANES), row), pl.BlockSpec((tm, LANES), row),
                  *[_layer_spec(a, layer) for a in (wa, qg, kvg, wq, wqr, wk, wv)]],
        out_specs=[pl.BlockSpec((tm, hp), row), pl.BlockSpec((tm, hp), row), pl.BlockSpec((tm, hp), row)],
        out_shape=[jax.ShapeDtypeStruct((t, hp), BF16)] * 3,
        compiler_params=_params("parallel"),
        name="mla_prep",
    )(xb, cos_t, sin_t, wa, qg, kvg, wq, wqr, wk, wv)


def _flash_kernel(q_ref, k_ref, v_ref, o_ref, *, blk):
    qi = pl.program_id(2)
    qs = (q_ref[:, :A_HEAD_PAD], q_ref[:, A_HEAD_PAD:])
    heads = tuple(slice(hh * A_HEAD_PAD, (hh + 1) * A_HEAD_PAD) for hh in range(2))
    row = lax.broadcasted_iota(jnp.int32, (blk, blk), 0)
    col = lax.broadcasted_iota(jnp.int32, (blk, blk), 1)

    def scores(j):
        kk = k_ref[pl.ds(pl.multiple_of(j * blk, blk), blk), :]
        return [_dot_nt(qs[hh], kk[:, heads[hh]]) for hh in range(2)]

    def update(carry, j, s, masked):
        vv = v_ref[pl.ds(pl.multiple_of(j * blk, blk), blk), :]
        out = []
        for hh in range(2):
            m, acc = carry[2 * hh:2 * hh + 2]
            sh = jnp.where(col <= row, s[hh], MASK_VALUE) if masked else s[hh]
            m_new = jnp.maximum(m, jnp.max(sh, axis=1, keepdims=True))
            p = jnp.exp2(sh - m_new)
            acc = jnp.exp2(m - m_new) * acc + _dot(p.astype(BF16), vv[:, heads[hh]])
            out += [m_new, acc]
        return tuple(out)

    def run(carry, first, count, diagonal):
        js = [first + t for t in range(count)]
        ss = [scores(j) for j in js]
        for t in range(count):
            carry = update(carry, js[t], ss[t], diagonal and t == count - 1)
        return carry

    carry = (jnp.full((blk, 1), -jnp.inf, F32), jnp.zeros((blk, A_HEAD_PAD), F32)) * 2
    groups = qi // ATTN_GROUP
    carry = lax.fori_loop(0, groups, lambda i, c: run(c, ATTN_GROUP * i, ATTN_GROUP, False), carry)
    first = groups * ATTN_GROUP
    tails = [functools.partial(run, first=first, count=r + 1, diagonal=True) for r in range(ATTN_GROUP)]
    _, acc0, _, acc1 = lax.switch(qi - first, tails, carry)
    o0 = acc0 / acc0[:, A_VDIM:A_VDIM + 1]
    o1 = acc1 / acc1[:, A_VDIM:A_VDIM + 1]
    lane = lax.broadcasted_iota(jnp.int32, o0.shape, 1)
    o_ref[...] = jnp.where(lane < A_VDIM, o0, pltpu.roll(o1, A_VDIM, 1))


def _mla_flash(q, k, v, batch, seq):
    blk = min(ATTN_TILE, seq)
    nq = seq // blk
    return pl.pallas_call(
        functools.partial(_flash_kernel, blk=blk),
        grid=(batch, A_HEADS // 2, nq),
        in_specs=[pl.BlockSpec((blk, 2 * A_HEAD_PAD), lambda b, h, i: (b * nq + i, h)),
                  pl.BlockSpec((seq, 2 * A_HEAD_PAD), lambda b, h, i: (b, h)),
                  pl.BlockSpec((seq, 2 * A_HEAD_PAD), lambda b, h, i: (b, h))],
        out_specs=pl.BlockSpec((blk, 2 * A_VDIM), lambda b, h, i: (b * nq + i, h)),
        out_shape=jax.ShapeDtypeStruct((batch * seq, BRANCH_W), F32),
        compiler_params=_params("parallel", "parallel", "arbitrary"),
        name="mla_flash",
    )(q, k, v)


def _cumsum_rows(x, period):
    pos = lax.broadcasted_iota(jnp.int32, x.shape, 0) % period
    s = 1
    while s < period:
        x = x + jnp.where(pos >= s, pltpu.roll(x, s, 0), 0.0)
        s *= 2
    return x


def _bdot(a, b):
    return lax.dot_general(a, b, (((2,), (1,)), ((0,), (0,))), preferred_element_type=F32)


def _bdot_nt(a, b):
    return lax.dot_general(a, b, (((2,), (2,)), ((0,), (0,))), preferred_element_type=F32)


def _unit_lower_inverse(low):
    c = low.shape[-1]
    eye = (lax.broadcasted_iota(jnp.int32, (c, c), 0) == lax.broadcasted_iota(jnp.int32, (c, c), 1)).astype(F32)
    inv = eye - low
    pw16 = low.astype(BF16)
    k = 2
    while k < c:
        pw16 = _bdot(pw16, pw16).astype(BF16)
        inv = inv + _bdot(inv.astype(BF16), pw16)
        k *= 2
    return inv


def _gdn_kernel(xb_ref, w_ref, convw_ref, alog_ref, dtb_ref, ng_ref, o_ref,
                state_ref, xbuf_ref, m_s, n_s, qp_s, o0_s, egl_s, out_s, *, tile):
    c = B_CHUNK
    n = tile // c
    nb = B_HEADS * n
    hd = B_HEADS * B_DK

    @pl.when(pl.program_id(1) == 0)
    def _():
        state_ref[...] = jnp.zeros_like(state_ref)
        xbuf_ref[0:SUBLANES, :] = jnp.zeros((SUBLANES, 3 * hd), F32)

    h = _dot(xb_ref[...], w_ref[...])
    xbuf_ref[SUBLANES:, :] = h[:, :3 * hd]
    conv = h[:, :3 * hd] * convw_ref[B_CONV - 1:B_CONV, :]
    for s in range(1, B_CONV):
        conv = conv + xbuf_ref[pl.ds(SUBLANES - s, tile), :] * convw_ref[B_CONV - 1 - s:B_CONV - s, :]
    xbuf_ref[0:SUBLANES, :] = h[tile - SUBLANES:, :3 * hd]
    qkv = _silu(conv)

    gates = h[:, 3 * hd:]
    g_all = -jnp.exp(alog_ref[...]) * _softplus(gates + dtb_ref[...])
    gcum = _cumsum_rows(g_all, c)
    beta_all = _sigmoid(gates)

    def per_head(fn):
        return jnp.concatenate([fn(hh) for hh in range(B_HEADS)], axis=0)

    def unit(x):
        return x * lax.rsqrt(jnp.sum(x * x, axis=-1, keepdims=True) + NORM_EPS)

    q = per_head(lambda hh: (unit(qkv[:, hh * B_DK:(hh + 1) * B_DK]) * (B_DK ** -0.5)).reshape(n, c, B_DK))
    k = per_head(lambda hh: unit(qkv[:, hd + hh * B_DK:hd + (hh + 1) * B_DK]).reshape(n, c, B_DK))
    v = per_head(lambda hh: qkv[:, 2 * hd + hh * B_DV:2 * hd + (hh + 1) * B_DV].reshape(n, c, B_DV))
    gc = per_head(lambda hh: gcum[:, hh:hh + 1].reshape(n, c, 1))
    beta = per_head(lambda hh: beta_all[:, B_HEADS + hh:B_HEADS + hh + 1].reshape(n, c, 1))

    ri = lax.broadcasted_iota(jnp.int32, (c, c), 0)
    ci = lax.broadcasted_iota(jnp.int32, (c, c), 1)
    tril = ci <= ri
    gcb = jnp.broadcast_to(gc, (nb, c, c))
    diff = gcb - jnp.swapaxes(gcb, 1, 2)
    decay = jnp.where(tril, jnp.exp(jnp.where(tril, diff, 0.0)), 0.0)
    k16 = k.astype(BF16)
    k_beta = k * beta
    low = jnp.where(ci < ri, _bdot_nt(k_beta.astype(BF16), k16) * decay, 0.0)
    inv = _unit_lower_inverse(low)
    eg = jnp.exp(gc)
    rhs = jnp.concatenate([v * beta, k_beta * eg], axis=2)
    uw16 = _bdot(inv.astype(BF16), rhs.astype(BF16)).astype(BF16)
    intra16 = (_bdot_nt(q.astype(BF16), k16) * decay).astype(BF16)
    g_last = gc[:, c - 1:c, :]
    kd_t = jnp.swapaxes(k * jnp.exp(g_last - gc), 1, 2).astype(BF16)
    kuw = _bdot(kd_t, uw16)
    iuw = _bdot(intra16, uw16)
    n_s[...] = kuw[:, :, :B_DV]
    m_s[...] = kuw[:, :, B_DV:].astype(BF16)
    o0_s[...] = iuw[:, :, :B_DV]
    qp_s[...] = (q * eg - iuw[:, :, B_DV:]).astype(BF16)
    egl_s[...] = jnp.broadcast_to(jnp.exp(g_last), (nb, 1, B_DV))

    heads = range(B_HEADS)
    st = [state_ref[hh] for hh in heads]
    for nn in range(n):
        st16 = [s.astype(BF16) for s in st]
        ms = [_dot(m_s[hh * n + nn], st16[hh]) for hh in heads]
        for hh in heads:
            out_s[nn * c:(nn + 1) * c, hh * B_DV:(hh + 1) * B_DV] = _dot(qp_s[hh * n + nn], st16[hh]) + o0_s[hh * n + nn]
        st = [st[hh] * egl_s[hh * n + nn] - ms[hh] + n_s[hh * n + nn] for hh in heads]
    for hh in heads:
        state_ref[hh] = st[hh]
        o_ref[:, hh * B_DV:(hh + 1) * B_DV] = _rms(out_s[:, hh * B_DV:(hh + 1) * B_DV], ng_ref[...])


def _gdn(layer, xb, w, conv_w, a_log, dt_bias, norm_g, batch, seq):
    tile = min(GDN_TILE, seq)
    nt = seq // tile
    hd = B_HEADS * B_DK
    nb = B_HEADS * (tile // B_CHUNK)
    return pl.pallas_call(
        functools.partial(_gdn_kernel, tile=tile),
        grid=(batch, nt),
        in_specs=[pl.BlockSpec((tile, D_MODEL), lambda b, i: (b * nt + i, 0)),
                  *[_layer_spec(a, layer) for a in (w, conv_w, a_log, dt_bias, norm_g)]],
        out_specs=pl.BlockSpec((tile, BRANCH_W), lambda b, i: (b * nt + i, 0)),
        out_shape=jax.ShapeDtypeStruct((batch * seq, BRANCH_W), F32),
        scratch_shapes=[pltpu.VMEM((B_HEADS, B_DK, B_DV), F32),
                        pltpu.VMEM((SUBLANES + tile, 3 * hd), F32),
                        pltpu.VMEM((nb, B_DK, B_DK), BF16),
                        pltpu.VMEM((nb, B_DK, B_DV), F32),
                        pltpu.VMEM((nb, B_CHUNK, B_DK), BF16),
                        pltpu.VMEM((nb, B_CHUNK, B_DV), F32),
                        pltpu.VMEM((nb, 1, B_DV), F32),
                        pltpu.VMEM((tile, hd), F32)],
        compiler_params=_params("arbitrary", "arbitrary"),
        name="gdn",
    )(xb, w, conv_w, a_log, dt_bias, norm_g)


def _hgrn_levels(tile):
    out = []
    lv = tile // 2
    while lv >= HGRN_SUB:
        out.append(lv)
        lv //= 2
    return tuple(out)


def _hgrn_level_map(tile):
    i = np.arange(tile)[:, None]
    j = np.arange(tile)[None, :]
    x = np.bitwise_xor(i, j)
    lev = np.where(j < i, np.floor(np.log2(np.maximum(x, 1))).astype(np.int32), -1)
    return jnp.asarray(lev, jnp.int32)


def _hgrn_kernel(xb_ref, w_ref, lb_ref, ng_ref, lev_ref, tri_ref, o_ref,
                 state_ref, q_s, k_s, v_s, cum_s, acc_s, *, tile):
    hd = C_HEADS * C_DK

    @pl.when(pl.program_id(1) == 0)
    def _():
        state_ref[...] = jnp.zeros_like(state_ref)

    h = _dot(xb_ref[...], w_ref[...])
    lev = lev_ref[...]
    sub_row = lax.broadcasted_iota(jnp.int32, (HGRN_SUB, C_DV), 0)

    lb = lb_ref[...]
    q_all = _silu(h[:, :hd]) * (C_DK ** -0.5)
    sig = _sigmoid(h[:, hd:2 * hd])
    log_f = jnp.log(jnp.maximum(lb + (1.0 - lb) * sig, 1e-30))
    k_all = (1.0 - lb) * (1.0 - sig)
    f_hi = log_f.astype(BF16)
    r1 = log_f - f_hi.astype(F32)
    f_mid = r1.astype(BF16)
    f_lo = (r1 - f_mid.astype(F32)).astype(BF16)
    parts = _dot(tri_ref[...], jnp.concatenate([f_hi, f_mid, f_lo], axis=1))
    cum_all = parts[:, :hd] + parts[:, hd:2 * hd] + parts[:, 2 * hd:]

    heads = range(C_HEADS)
    cols = [slice(hh * C_DK, (hh + 1) * C_DK) for hh in heads]
    q = [q_all[:, sl] for sl in cols]
    k = [k_all[:, sl] for sl in cols]
    cum = [cum_all[:, sl] for sl in cols]
    v16 = [h[:, 2 * hd + hh * C_DV:2 * hd + (hh + 1) * C_DV].astype(BF16) for hh in heads]
    for hh in heads:
        q_s[hh] = q[hh]
        k_s[hh] = k[hh]
        v_s[hh] = h[:, 2 * hd + hh * C_DV:2 * hd + (hh + 1) * C_DV]
        cum_s[hh] = cum[hh]

    state_t = [state_ref[hh] for hh in heads]
    for hh in heads:
        acc_s[hh] = _dot_nt((q[hh] * jnp.exp(cum[hh])).astype(BF16), state_t[hh].astype(BF16))
    for hh in heads:
        cum_last = cum[hh][tile - 1:tile, :]
        k_dec = k[hh] * jnp.exp(cum_last - cum[hh])
        state_ref[hh] = state_t[hh] * jnp.exp(cum_last) + _dot_tn(v16[hh], k_dec.astype(BF16))

    for lv in _hgrn_levels(tile):
        if lv < HGRN_DENSE_MIN:
            continue
        blocks = tile // (2 * lv)
        shape = (blocks, 2 * lv, C_DK)
        s_lv = []
        for hh in heads:
            q3, k3, c3 = q[hh].reshape(shape), k[hh].reshape(shape), cum[hh].reshape(shape)
            ref = c3[:, lv:lv + 1, :]
            qt = q3[:, lv:, :] * jnp.exp(c3[:, lv:, :] - ref)
            kt = k3[:, :lv, :] * jnp.exp(ref - c3[:, :lv, :])
            s_lv.append(_bdot_nt(qt.astype(BF16), kt.astype(BF16)).astype(BF16))
        for hh in heads:
            o_lv = _bdot(s_lv[hh], v16[hh].reshape(shape)[:, :lv, :])
            for bb in range(blocks):
                acc_s[hh, bb * 2 * lv + lv:(bb + 1) * 2 * lv, :] += o_lv[bb]
    span = 2 * HGRN_DENSE_MIN
    shape = (tile // span, span, C_DK)
    lev_s = lev[:span, :span]
    scores = [jnp.zeros((tile // span, span, span), F32) for _ in heads]
    for lv in _hgrn_levels(tile):
        if lv >= HGRN_DENSE_MIN:
            continue
        for hh in heads:
            c3 = cum[hh].reshape(shape)
            c4 = cum[hh].reshape(tile // (2 * lv), 2 * lv, C_DK)
            ref = jnp.broadcast_to(c4[:, lv:lv + 1, :], c4.shape).reshape(shape)
            qt = q[hh].reshape(shape) * jnp.exp(jnp.minimum(c3 - ref, 0.0))
            kt = k[hh].reshape(shape) * jnp.exp(jnp.minimum(ref - c3, 0.0))
            s_lv = _bdot_nt(qt.astype(BF16), kt.astype(BF16))
            scores[hh] = jnp.where(lev_s == int(np.log2(lv)), s_lv, scores[hh])
    for hh in heads:
        acc_s[hh] += _bdot(scores[hh].astype(BF16), v16[hh].reshape(shape)).reshape(tile, C_DV)

    for hh in heads:
        def group(gi, _):
            base = pl.multiple_of(gi * (HGRN_GROUP * HGRN_SUB), HGRN_GROUP * HGRN_SUB)
            blocks = [pl.ds(base + ss * HGRN_SUB, HGRN_SUB) for ss in range(HGRN_GROUP)]
            qv = [q_s[hh, r, :] for r in blocks]
            cv = [cum_s[hh, r, :] for r in blocks]
            acc = [acc_s[hh, r, :] for r in blocks]
            for j in range(HGRN_SUB):
                for ss in range(HGRN_GROUP):
                    rj = pl.ds(base + ss * HGRN_SUB + j, 1)
                    e = jnp.exp(jnp.minimum(cv[ss] - cum_s[hh, rj, :], 0.0))
                    sc = jnp.sum(qv[ss] * k_s[hh, rj, :] * e, axis=1, keepdims=True)
                    acc[ss] = acc[ss] + jnp.where(sub_row >= j, sc, 0.0) * v_s[hh, rj, :]
            for ss in range(HGRN_GROUP):
                acc_s[hh, blocks[ss], :] = acc[ss]
            return 0

        lax.fori_loop(0, tile // (HGRN_GROUP * HGRN_SUB), group, 0)
        o_ref[:, cols[hh]] = _rms(acc_s[hh], ng_ref[...])


def _hgrn(layer, xb, w, lower_bound, norm_g, batch, seq):
    tile = min(REC_TILE, seq)
    nt = seq // tile
    lev = _hgrn_level_map(tile)
    tri = jnp.asarray(np.tril(np.ones((tile, tile), np.float32)), BF16)
    return pl.pallas_call(
        functools.partial(_hgrn_kernel, tile=tile),
        grid=(batch, nt),
        in_specs=[pl.BlockSpec((tile, D_MODEL), lambda b, i: (b * nt + i, 0)),
                  *[_layer_spec(a, layer) for a in (w, lower_bound, norm_g)],
                  _const_spec(lev.shape), _const_spec(tri.shape)],
        out_specs=pl.BlockSpec((tile, BRANCH_W), lambda b, i: (b * nt + i, 0)),
        out_shape=jax.ShapeDtypeStruct((batch * seq, BRANCH_W), F32),
        scratch_shapes=[pltpu.VMEM((C_HEADS, C_DV, C_DK), F32)]
                       + [pltpu.VMEM((C_HEADS, tile, C_DK), F32) for _ in range(5)],
        compiler_params=_params("arbitrary", "arbitrary"),
        name="hgrn2",
    )(xb, w, lower_bound, norm_g, lev, tri)


def _merge_kernel(x_ref, xb_ref, oa_ref, ob_ref, oc_ref, p_ref, wz_ref, wg_ref, wbr_ref, wout_ref,
                  pgate_ref, pproj_ref, lng_ref, lnb_ref, xo_ref, xbo_ref):
    xb = xb_ref[...]
    z = _dot(xb, wz_ref[...])
    merged = None
    for n, o_ref in enumerate((oa_ref, ob_ref, oc_ref)):
        y = o_ref[...] * _silu(z[:, n * BRANCH_W:(n + 1) * BRANCH_W])
        proj = _dot(y.astype(BF16), wbr_ref[n])
        gate = _sigmoid(_dot(xb, wg_ref[:, n * D_MODEL:(n + 1) * D_MODEL]))
        merged = gate * proj if merged is None else merged + gate * proj
    r = DEEPNORM_ALPHA * x_ref[...] + _dot(merged.astype(BF16), wout_ref[...])
    ple = _dot(p_ref[...].astype(BF16), pproj_ref[...])
    r = r + _sigmoid(_dot(r.astype(BF16), pgate_ref[...])) * ple
    mu = jnp.mean(r, axis=-1, keepdims=True)
    var = jnp.mean(jnp.square(r - mu), axis=-1, keepdims=True)
    xn = (r - mu) * lax.rsqrt(var + NORM_EPS) * lng_ref[...] + lnb_ref[...]
    xo_ref[...] = xn
    xbo_ref[...] = xn.astype(BF16)


def _merge(layer, x, xb, oa, ob, oc, p, wz, wg, wbr, wout, pgate, pproj, lng, lnb):
    t = x.shape[0]
    seq = p.shape[2]
    tm = min(TOKEN_TILE, seq)
    per_seq = seq // tm
    row = lambda i: (i, 0)
    return pl.pallas_call(
        _merge_kernel,
        grid=(t // tm,),
        in_specs=[pl.BlockSpec((tm, D_MODEL), row), pl.BlockSpec((tm, D_MODEL), row),
                  pl.BlockSpec((tm, BRANCH_W), row), pl.BlockSpec((tm, BRANCH_W), row),
                  pl.BlockSpec((tm, BRANCH_W), row),
                  pl.BlockSpec((None, None, tm, PLE_DIM), lambda i: (layer, i // per_seq, i % per_seq, 0)),
                  *[_layer_spec(a, layer) for a in (wz, wg, wbr, wout, pgate, pproj, lng, lnb)]],
        out_specs=[pl.BlockSpec((tm, D_MODEL), row), pl.BlockSpec((tm, D_MODEL), row)],
        out_shape=[jax.ShapeDtypeStruct((t, D_MODEL), F32), jax.ShapeDtypeStruct((t, D_MODEL), BF16)],
        compiler_params=_params("parallel"),
        name="merge",
    )(x, xb, oa, ob, oc, p, wz, wg, wbr, wout, pgate, pproj, lng, lnb)


def _cols(w, idx):
    return w[..., IN_OFFSETS[idx]:IN_OFFSETS[idx + 1]]


def _rot_cols(w):
    half = w.shape[-1] // 2
    return jnp.concatenate([-w[..., half:], w[..., :half]], axis=-1)


def _pad_last(w, left, total):
    return jnp.pad(w, ((0, 0),) * (w.ndim - 1) + ((left, total - left - w.shape[-1]),))


def _mla_weights(w_in, w_uq, w_ukv):
    depth = w_in.shape[0]
    w_kr = _cols(w_in, 2)
    wa = jnp.concatenate([_cols(w_in, 0), _cols(w_in, 1), _pad_last(w_kr, A_NOPE, LANES),
                          _pad_last(_rot_cols(w_kr), A_NOPE, LANES)], axis=-1).astype(BF16)
    uq = w_uq.reshape(depth, A_Q_LORA, A_HEADS, A_NOPE + A_ROPE)
    wq = _pad_last(uq, 0, A_HEAD_PAD).reshape(depth, A_Q_LORA, A_HEADS * A_HEAD_PAD).astype(BF16)
    uq_rot = jnp.concatenate([jnp.zeros_like(uq[..., :A_NOPE]), _rot_cols(uq[..., A_NOPE:])], axis=-1)
    wqr = _pad_last(uq_rot, 0, A_HEAD_PAD).reshape(depth, A_Q_LORA, A_HEADS * A_HEAD_PAD).astype(BF16)
    ukv = w_ukv.reshape(depth, A_KV_LORA, A_HEADS, A_NOPE + A_VDIM)
    wk = _pad_last(ukv[..., :A_NOPE], 0, A_HEAD_PAD).reshape(depth, A_KV_LORA, A_HEADS * A_HEAD_PAD).astype(BF16)
    wv = _pad_last(ukv[..., A_NOPE:], 0, A_HEAD_PAD).reshape(depth, A_KV_LORA, A_HEADS * A_HEAD_PAD).astype(BF16)
    return wa, wq, wqr, wk, wv


def _rope_tables(positions):
    inv = ROPE_THETA ** (-jnp.arange(0, A_ROPE, 2, dtype=F32) / A_ROPE)
    ang = positions.astype(F32).reshape(-1, 1) * inv
    cos, sin = lax.optimization_barrier((jnp.cos(ang), jnp.sin(ang)))
    place = lambda t: _pad_last(jnp.concatenate([t, t], axis=-1), A_NOPE, LANES)
    return place(cos), place(sin)


def _lower_bounds(logits):
    pr = jax.nn.softmax(logits.astype(F32), axis=0)
    return jnp.clip(jnp.cumsum(pr, axis=0) - pr[0:1], 0.0, 1.0 - 1e-6)


def _rows(v):
    return v.astype(F32)[:, None, :]


def kernel(x, p, positions, w_in, a_q_norm, a_w_uq, a_kv_norm, a_w_ukv, b_conv, b_a_log, b_dt_bias, b_norm,
           c_lb_logits, c_norm, w_branch, w_out, ple_proj, ple_gate, ln_g, ln_b):
    batch, seq, _ = x.shape
    t = batch * seq
    depth = w_in.shape[0]
    cos_t, sin_t = _rope_tables(positions)

    wa, wq, wqr, wk, wv = _mla_weights(w_in, a_w_uq, a_w_ukv)
    gate_cols = jnp.concatenate([_cols(w_in, 5), _cols(w_in, 6)], axis=-1)
    wb = jnp.concatenate([_cols(w_in, 4), _pad_last(gate_cols, 0, LANES)], axis=-1).astype(BF16)
    wc = jnp.concatenate([_cols(w_in, 8), _cols(w_in, 9), _cols(w_in, 10)], axis=-1).astype(BF16)
    wz = jnp.concatenate([_cols(w_in, 3), _cols(w_in, 7), _cols(w_in, 11)], axis=-1).astype(BF16)
    wg = _cols(w_in, 12).astype(BF16)
    wbr, wout, pgate, pproj = (w.astype(BF16) for w in (w_branch, w_out, ple_gate, ple_proj))
    qg, kvg, ng_b, ng_c, lng, lnb = (_rows(v) for v in (a_q_norm, a_kv_norm, b_norm, c_norm, ln_g, ln_b))
    conv_w = b_conv.reshape(depth, B_CONV, -1)
    a_log = _rows(_pad_last(b_a_log, 0, LANES))
    dt_bias = _rows(_pad_last(b_dt_bias, 0, LANES))
    lower_bounds = _rows(_lower_bounds(c_lb_logits))

    xf = x.reshape(t, D_MODEL)
    xb = xf.astype(BF16)
    for i in range(depth):
        q, k, v = _mla_prep(i, xb, cos_t, sin_t, wa, qg, kvg, wq, wqr, wk, wv)
        o_a = _mla_flash(q, k, v, batch, seq)
        o_b = _gdn(i, xb, wb, conv_w, a_log, dt_bias, ng_b, batch, seq)
        o_c = _hgrn(i, xb, wc, lower_bounds, ng_c, batch, seq)
        xf, xb = _merge(i, xf, xb, o_a, o_b, o_c, p, wz, wg, wbr, wout, pgate, pproj, lng, lnb)
    return xf.reshape(batch, seq, D_MODEL)
```

```python
import functools

import numpy as np
import jax
import jax.numpy as jnp
from jax import lax
from jax.experimental import pallas as pl
from jax.experimental.pallas import tpu as pltpu

F32 = jnp.float32
BF16 = jnp.bfloat16

D_MODEL = 1024
DEPTH = 4
PLE_DIM = 256
BRANCH_W = 512
N_BRANCH = 3
A_HEADS = 8
A_NOPE = 64
A_ROPE = 32
A_VDIM = 64
A_Q_LORA = 256
A_KV_LORA = 128
A_SCALE = (A_NOPE + A_ROPE) ** -0.5
ROPE_THETA = 10000.0
B_HEADS = 4
B_DK = 128
B_DV = 128
B_CONV = 4
B_CHUNK = 64
C_HEADS = 4
C_DK = 128
C_DV = 128
NORM_EPS = 1e-6
MASK_VALUE = -1e30
DEEPNORM_ALPHA = (2.0 * DEPTH) ** 0.25
LOG2_E = 1.4426950408889634

IN_SPLITS = (A_Q_LORA, A_KV_LORA, A_ROPE, BRANCH_W, 3 * B_HEADS * B_DK, B_HEADS, B_HEADS, BRANCH_W,
             C_HEADS * C_DK, C_HEADS * C_DK, C_HEADS * C_DV, BRANCH_W, N_BRANCH * D_MODEL)
IN_OFFSETS = tuple(int(o) for o in np.cumsum((0,) + IN_SPLITS))

LANES = 128
SUBLANES = 8
A_HEAD_PAD = LANES
VMEM_LIMIT = 56 * 1024 * 1024

TOKEN_TILE = 512
ATTN_TILE = 512
ATTN_GROUP = 4
REC_TILE = 256
GDN_TILE = 512
HGRN_SUB = SUBLANES
HGRN_GROUP = 32
HGRN_DENSE_MIN = 16


def _sigmoid(x):
    return 1.0 / (1.0 + jnp.exp(-x))


def _silu(x):
    return x * _sigmoid(x)


def _softplus(x):
    return jnp.maximum(x, 0.0) + jnp.log(1.0 + jnp.exp(-jnp.abs(x)))


def _rms(x, g):
    return x * lax.rsqrt(jnp.mean(x * x, axis=-1, keepdims=True) + NORM_EPS) * g


def _dot(a, b):
    return jnp.dot(a, b, preferred_element_type=F32)


def _dot_nt(a, b):
    return lax.dot_general(a, b, (((1,), (1,)), ((), ())), preferred_element_type=F32)


def _dot_tn(a, b):
    return lax.dot_general(a, b, (((0,), (0,)), ((), ())), preferred_element_type=F32)


def _const_spec(shape):
    nd = len(shape)
    return pl.BlockSpec(shape, lambda *_: (0,) * nd, pipeline_mode=pl.Buffered(1))


def _layer_spec(arr, layer):
    nd = arr.ndim - 1
    return pl.BlockSpec((None,) + arr.shape[1:], lambda *_: (layer,) + (0,) * nd, pipeline_mode=pl.Buffered(1))


def _params(*sem):
    return pltpu.CompilerParams(dimension_semantics=sem, vmem_limit_bytes=VMEM_LIMIT)


def _mla_prep_kernel(xb_ref, cos_ref, sin_ref, wa_ref, qg_ref, kvg_ref, wq_ref, wqr_ref, wk_ref, wv_ref,
                     q_ref, k_ref, v_ref):
    h = _dot(xb_ref[...], wa_ref[...])
    cq = h[:, :A_Q_LORA]
    ckv = h[:, A_Q_LORA:A_Q_LORA + A_KV_LORA]
    kr = h[:, A_Q_LORA + A_KV_LORA:A_Q_LORA + A_KV_LORA + LANES]
    kr_rot = h[:, A_Q_LORA + A_KV_LORA + LANES:]
    cqn = _rms(cq, qg_ref[...]).astype(BF16)
    ckvn = _rms(ckv, kvg_ref[...]).astype(BF16)
    cos = cos_ref[...]
    sin = sin_ref[...]
    lane = lax.broadcasted_iota(jnp.int32, cos.shape, 1)
    cq_tab = jnp.where(lane < A_NOPE, 1.0, cos) * (A_SCALE * LOG2_E)
    sq_tab = sin * (A_SCALE * LOG2_E)
    cq_tab = jnp.concatenate([cq_tab] * A_HEADS, axis=1)
    sq_tab = jnp.concatenate([sq_tab] * A_HEADS, axis=1)
    q = _dot(cqn, wq_ref[...]) * cq_tab + _dot(cqn, wqr_ref[...]) * sq_tab
    q_ref[...] = q.astype(BF16)
    k_rope = kr * cos + kr_rot * sin
    k = _dot(ckvn, wk_ref[...]) + jnp.concatenate([k_rope] * A_HEADS, axis=1)
    k_ref[...] = k.astype(BF16)
    v = _dot(ckvn, wv_ref[...])
    vlane = lax.broadcasted_iota(jnp.int32, v.shape, 1) % A_HEAD_PAD
    v_ref[...] = jnp.where(vlane == A_VDIM, 1.0, v).astype(BF16)


def _mla_prep(layer, xb, cos_t, sin_t, wa, qg, kvg, wq, wqr, wk, wv):
    t = xb.shape[0]
    tm = min(TOKEN_TILE, t)
    row = lambda i: (i, 0)
    hp = A_HEADS * A_HEAD_PAD
    return pl.pallas_call(
        _mla_prep_kernel,
        grid=(t // tm,),
        in_specs=[pl.BlockSpec((tm, D_MODEL), row), pl.BlockSpec((tm, LANES), row), pl.BlockSpec((tm, LANES), row),
                  *[_layer_spec(a, layer) for a in (wa, qg, kvg, wq, wqr, wk, wv)]],
        out_specs=[pl.BlockSpec((tm, hp), row), pl.BlockSpec((tm, hp), row), pl.BlockSpec((tm, hp), row)],
        out_shape=[jax.ShapeDtypeStruct((t, hp), BF16)] * 3,
        compiler_params=_params("parallel"),
        name="mla_prep",
    )(xb, cos_t, sin_t, wa, qg, kvg, wq, wqr, wk, wv)


def _flash_kernel(q_ref, k_ref, v_ref, o_ref, *, blk):
    qi = pl.program_id(2)
    qs = (q_ref[:, :A_HEAD_PAD], q_ref[:, A_HEAD_PAD:])
    heads = tuple(slice(hh * A_HEAD_PAD, (hh + 1) * A_HEAD_PAD) for hh in range(2))
    row = lax.broadcasted_iota(jnp.int32, (blk, blk), 0)
    col = lax.broadcasted_iota(jnp.int32, (blk, blk), 1)

    def scores(j):
        kk = k_ref[pl.ds(pl.multiple_of(j * blk, blk), blk), :]
        return [_dot_nt(qs[hh], kk[:, heads[hh]]) for hh in range(2)]

    def update(carry, j, s, masked):
        vv = v_ref[pl.ds(pl.multiple_of(j * blk, blk), blk), :]
        out = []
        for hh in range(2):
            m, acc = carry[2 * hh:2 * hh + 2]
            sh = jnp.where(col <= row, s[hh], MASK_VALUE) if masked else s[hh]
            m_new = jnp.maximum(m, jnp.max(sh, axis=1, keepdims=True))
            p = jnp.exp2(sh - m_new)
            acc = jnp.exp2(m - m_new) * acc + _dot(p.astype(BF16), vv[:, heads[hh]])
            out += [m_new, acc]
        return tuple(out)

    def run(carry, first, count, diagonal):
        js = [first + t for t in range(count)]
        ss = [scores(j) for j in js]
        for t in range(count):
            carry = update(carry, js[t], ss[t], diagonal and t == count - 1)
        return carry

    carry = (jnp.full((blk, 1), -jnp.inf, F32), jnp.zeros((blk, A_HEAD_PAD), F32)) * 2
    groups = qi // ATTN_GROUP
    carry = lax.fori_loop(0, groups, lambda i, c: run(c, ATTN_GROUP * i, ATTN_GROUP, False), carry)
    first = groups * ATTN_GROUP
    tails = [functools.partial(run, first=first, count=r + 1, diagonal=True) for r in range(ATTN_GROUP)]
    _, acc0, _, acc1 = lax.switch(qi - first, tails, carry)
    o0 = acc0 / acc0[:, A_VDIM:A_VDIM + 1]
    o1 = acc1 / acc1[:, A_VDIM:A_VDIM + 1]
    lane = lax.broadcasted_iota(jnp.int32, o0.shape, 1)
    o_ref[...] = jnp.where(lane < A_VDIM, o0, pltpu.roll(o1, A_VDIM, 1))


def _mla_flash(q, k, v, batch, seq):
    blk = min(ATTN_TILE, seq)
    nq = seq // blk
    return pl.pallas_call(
        functools.partial(_flash_kernel, blk=blk),
        grid=(batch, A_HEADS // 2, nq),
        in_specs=[pl.BlockSpec((blk, 2 * A_HEAD_PAD), lambda b, h, i: (b * nq + i, h)),
                  pl.BlockSpec((seq, 2 * A_HEAD_PAD), lambda b, h, i: (b, h)),
                  pl.BlockSpec((seq, 2 * A_HEAD_PAD), lambda b, h, i: (b, h))],
        out_specs=pl.BlockSpec((blk, 2 * A_VDIM), lambda b, h, i: (b * nq + i, h)),
        out_shape=jax.ShapeDtypeStruct((batch * seq, BRANCH_W), F32),
        compiler_params=_params("parallel", "parallel", "arbitrary"),
        name="mla_flash",
    )(q, k, v)


def _cumsum_rows(x, period):
    pos = lax.broadcasted_iota(jnp.int32, x.shape, 0) % period
    s = 1
    while s < period:
        x = x + jnp.where(pos >= s, pltpu.roll(x, s, 0), 0.0)
        s *= 2
    return x


def _bdot(a, b):
    return lax.dot_general(a, b, (((2,), (1,)), ((0,), (0,))), preferred_element_type=F32)


def _bdot_nt(a, b):
    return lax.dot_general(a, b, (((2,), (2,)), ((0,), (0,))), preferred_element_type=F32)


def _unit_lower_inverse(low):
    c = low.shape[-1]
    eye = (lax.broadcasted_iota(jnp.int32, (c, c), 0) == lax.broadcasted_iota(jnp.int32, (c, c), 1)).astype(F32)
    inv = eye - low
    pw16 = low.astype(BF16)
    k = 2
    while k < c:
        pw16 = _bdot(pw16, pw16).astype(BF16)
        inv = inv + _bdot(inv.astype(BF16), pw16)
        k *= 2
    return inv


def _gdn_kernel(xb_ref, w_ref, convw_ref, alog_ref, dtb_ref, ng_ref, o_ref,
                state_ref, xbuf_ref, m_s, n_s, qp_s, o0_s, egl_s, out_s, *, tile):
    c = B_CHUNK
    n = tile // c
    nb = B_HEADS * n
    hd = B_HEADS * B_DK

    @pl.when(pl.program_id(1) == 0)
    def _():
        state_ref[...] = jnp.zeros_like(state_ref)
        xbuf_ref[0:SUBLANES, :] = jnp.zeros((SUBLANES, 3 * hd), F32)

    h = _dot(xb_ref[...], w_ref[...])
    xbuf_ref[SUBLANES:, :] = h[:, :3 * hd]
    conv = h[:, :3 * hd] * convw_ref[B_CONV - 1:B_CONV, :]
    for s in range(1, B_CONV):
        conv = conv + xbuf_ref[pl.ds(SUBLANES - s, tile), :] * convw_ref[B_CONV - 1 - s:B_CONV - s, :]
    xbuf_ref[0:SUBLANES, :] = h[tile - SUBLANES:, :3 * hd]
    qkv = _silu(conv)

    gates = h[:, 3 * hd:]
    g_all = -jnp.exp(alog_ref[...]) * _softplus(gates + dtb_ref[...])
    gcum = _cumsum_rows(g_all, c)
    beta_all = _sigmoid(gates)

    def per_head(fn):
        return jnp.concatenate([fn(hh) for hh in range(B_HEADS)], axis=0)

    def unit(x):
        return x * lax.rsqrt(jnp.sum(x * x, axis=-1, keepdims=True) + NORM_EPS)

    q = per_head(lambda hh: (unit(qkv[:, hh * B_DK:(hh + 1) * B_DK]) * (B_DK ** -0.5)).reshape(n, c, B_DK))
    k = per_head(lambda hh: unit(qkv[:, hd + hh * B_DK:hd + (hh + 1) * B_DK]).reshape(n, c, B_DK))
    v = per_head(lambda hh: qkv[:, 2 * hd + hh * B_DV:2 * hd + (hh + 1) * B_DV].reshape(n, c, B_DV))
    gc = per_head(lambda hh: gcum[:, hh:hh + 1].reshape(n, c, 1))
    beta = per_head(lambda hh: beta_all[:, B_HEADS + hh:B_HEADS + hh + 1].reshape(n, c, 1))

    ri = lax.broadcasted_iota(jnp.int32, (c, c), 0)
    ci = lax.broadcasted_iota(jnp.int32, (c, c), 1)
    tril = ci <= ri
    gcb = jnp.broadcast_to(gc, (nb, c, c))
    diff = gcb - jnp.swapaxes(gcb, 1, 2)
    decay = jnp.where(tril, jnp.exp(jnp.where(tril, diff, 0.0)), 0.0)
    k16 = k.astype(BF16)
    k_beta = k * beta
    low = jnp.where(ci < ri, _bdot_nt(k_beta.astype(BF16), k16) * decay, 0.0)
    inv = _unit_lower_inverse(low)
    eg = jnp.exp(gc)
    rhs = jnp.concatenate([v * beta, k_beta * eg], axis=2)
    uw16 = _bdot(inv.astype(BF16), rhs.astype(BF16)).astype(BF16)
    intra16 = (_bdot_nt(q.astype(BF16), k16) * decay).astype(BF16)
    g_last = gc[:, c - 1:c, :]
    kd_t = jnp.swapaxes(k * jnp.exp(g_last - gc), 1, 2).astype(BF16)
    kuw = _bdot(kd_t, uw16)
    iuw = _bdot(intra16, uw16)
    n_s[...] = kuw[:, :, :B_DV]
    m_s[...] = kuw[:, :, B_DV:].astype(BF16)
    o0_s[...] = iuw[:, :, :B_DV]
    qp_s[...] = (q * eg - iuw[:, :, B_DV:]).astype(BF16)
    egl_s[...] = jnp.broadcast_to(jnp.exp(g_last), (nb, 1, B_DV))

    heads = range(B_HEADS)
    st = [state_ref[hh] for hh in heads]
    for nn in range(n):
        st16 = [s.astype(BF16) for s in st]
        ms = [_dot(m_s[hh * n + nn], st16[hh]) for hh in heads]
        for hh in heads:
            out_s[nn * c:(nn + 1) * c, hh * B_DV:(hh + 1) * B_DV] = _dot(qp_s[hh * n + nn], st16[hh]) + o0_s[hh * n + nn]
        st = [st[hh] * egl_s[hh * n + nn] - ms[hh] + n_s[hh * n + nn] for hh in heads]
    for hh in heads:
        state_ref[hh] = st[hh]
        o_ref[:, hh * B_DV:(hh + 1) * B_DV] = _rms(out_s[:, hh * B_DV:(hh + 1) * B_DV], ng_ref[...])


def _gdn(layer, xb, w, conv_w, a_log, dt_bias, norm_g, batch, seq):
    tile = min(GDN_TILE, seq)
    nt = seq // tile
    hd = B_HEADS * B_DK
    nb = B_HEADS * (tile // B_CHUNK)
    return pl.pallas_call(
        functools.partial(_gdn_kernel, tile=tile),
        grid=(batch, nt),
        in_specs=[pl.BlockSpec((tile, D_MODEL), lambda b, i: (b * nt + i, 0)),
                  *[_layer_spec(a, layer) for a in (w, conv_w, a_log, dt_bias, norm_g)]],
        out_specs=pl.BlockSpec((tile, BRANCH_W), lambda b, i: (b * nt + i, 0)),
        out_shape=jax.ShapeDtypeStruct((batch * seq, BRANCH_W), F32),
        scratch_shapes=[pltpu.VMEM((B_HEADS, B_DK, B_DV), F32),
                        pltpu.VMEM((SUBLANES + tile, 3 * hd), F32),
                        pltpu.VMEM((nb, B_DK, B_DK), BF16),
                        pltpu.VMEM((nb, B_DK, B_DV), F32),
                        pltpu.VMEM((nb, B_CHUNK, B_DK), BF16),
                        pltpu.VMEM((nb, B_CHUNK, B_DV), F32),
                        pltpu.VMEM((nb, 1, B_DV), F32),
                        pltpu.VMEM((tile, hd), F32)],
        compiler_params=_params("arbitrary", "arbitrary"),
        name="gdn",
    )(xb, w, conv_w, a_log, dt_bias, norm_g)


def _hgrn_levels(tile):
    out = []
    lv = tile // 2
    while lv >= HGRN_SUB:
        out.append(lv)
        lv //= 2
    return tuple(out)


def _hgrn_level_map(tile):
    i = np.arange(tile)[:, None]
    j = np.arange(tile)[None, :]
    x = np.bitwise_xor(i, j)
    lev = np.where(j < i, np.floor(np.log2(np.maximum(x, 1))).astype(np.int32), -1)
    return jnp.asarray(lev, jnp.int32)


def _hgrn_kernel(xb_ref, w_ref, lb_ref, ng_ref, lev_ref, tri_ref, o_ref,
                 state_ref, q_s, k_s, v_s, cum_s, acc_s, *, tile):
    hd = C_HEADS * C_DK

    @pl.when(pl.program_id(1) == 0)
    def _():
        state_ref[...] = jnp.zeros_like(state_ref)

    h = _dot(xb_ref[...], w_ref[...])
    lev = lev_ref[...]
    sub_row = lax.broadcasted_iota(jnp.int32, (HGRN_SUB, C_DV), 0)

    lb = lb_ref[...]
    q_all = _silu(h[:, :hd]) * (C_DK ** -0.5)
    sig = _sigmoid(h[:, hd:2 * hd])
    log_f = jnp.log(jnp.maximum(lb + (1.0 - lb) * sig, 1e-30))
    k_all = (1.0 - lb) * (1.0 - sig)
    f_hi = log_f.astype(BF16)
    r1 = log_f - f_hi.astype(F32)
    f_mid = r1.astype(BF16)
    f_lo = (r1 - f_mid.astype(F32)).astype(BF16)
    parts = _dot(tri_ref[...], jnp.concatenate([f_hi, f_mid, f_lo], axis=1))
    cum_all = parts[:, :hd] + parts[:, hd:2 * hd] + parts[:, 2 * hd:]

    heads = range(C_HEADS)
    cols = [slice(hh * C_DK, (hh + 1) * C_DK) for hh in heads]
    q = [q_all[:, sl] for sl in cols]
    k = [k_all[:, sl] for sl in cols]
    cum = [cum_all[:, sl] for sl in cols]
    v16 = [h[:, 2 * hd + hh * C_DV:2 * hd + (hh + 1) * C_DV].astype(BF16) for hh in heads]
    for hh in heads:
        q_s[hh] = q[hh]
        k_s[hh] = k[hh]
        v_s[hh] = h[:, 2 * hd + hh * C_DV:2 * hd + (hh + 1) * C_DV]
        cum_s[hh] = cum[hh]

    state_t = [state_ref[hh] for hh in heads]
    for hh in heads:
        acc_s[hh] = _dot_nt((q[hh] * jnp.exp(cum[hh])).astype(BF16), state_t[hh].astype(BF16))
    for hh in heads:
        cum_last = cum[hh][tile - 1:tile, :]
        k_dec = k[hh] * jnp.exp(cum_last - cum[hh])
        state_ref[hh] = state_t[hh] * jnp.exp(cum_last) + _dot_tn(v16[hh], k_dec.astype(BF16))

    for lv in _hgrn_levels(tile):
        if lv < HGRN_DENSE_MIN:
            continue
        blocks = tile // (2 * lv)
        shape = (blocks, 2 * lv, C_DK)
        s_lv = []
        for hh in heads:
            q3, k3, c3 = q[hh].reshape(shape), k[hh].reshape(shape), cum[hh].reshape(shape)
            ref = c3[:, lv:lv + 1, :]
            qt = q3[:, lv:, :] * jnp.exp(c3[:, lv:, :] - ref)
            kt = k3[:, :lv, :] * jnp.exp(ref - c3[:, :lv, :])
            s_lv.append(_bdot_nt(qt.astype(BF16), kt.astype(BF16)).astype(BF16))
        for hh in heads:
            o_lv = _bdot(s_lv[hh], v16[hh].reshape(shape)[:, :lv, :])
            for bb in range(blocks):
                acc_s[hh, bb * 2 * lv + lv:(bb + 1) * 2 * lv, :] += o_lv[bb]
    span = 2 * HGRN_DENSE_MIN
    shape = (tile // span, span, C_DK)
    lev_s = lev[:span, :span]
    scores = [jnp.zeros((tile // span, span, span), F32) for _ in heads]
    for lv in _hgrn_levels(tile):
        if lv >= HGRN_DENSE_MIN:
            continue
        for hh in heads:
            c3 = cum[hh].reshape(shape)
            c4 = cum[hh].reshape(tile // (2 * lv), 2 * lv, C_DK)
            ref = jnp.broadcast_to(c4[:, lv:lv + 1, :], c4.shape).reshape(shape)
            qt = q[hh].reshape(shape) * jnp.exp(jnp.minimum(c3 - ref, 0.0))
            kt = k[hh].reshape(shape) * jnp.exp(jnp.minimum(ref - c3, 0.0))
            s_lv = _bdot_nt(qt.astype(BF16), kt.astype(BF16))
            scores[hh] = jnp.where(lev_s == int(np.log2(lv)), s_lv, scores[hh])
    for hh in heads:
        acc_s[hh] += _bdot(scores[hh].astype(BF16), v16[hh].reshape(shape)).reshape(tile, C_DV)

    for hh in heads:
        def group(gi, _):
            base = pl.multiple_of(gi * (HGRN_GROUP * HGRN_SUB), HGRN_GROUP * HGRN_SUB)
            blocks = [pl.ds(base + ss * HGRN_SUB, HGRN_SUB) for ss in range(HGRN_GROUP)]
            qv = [q_s[hh, r, :] for r in blocks]
            cv = [cum_s[hh, r, :] for r in blocks]
            acc = [acc_s[hh, r, :] for r in blocks]
            for j in range(HGRN_SUB):
                for ss in range(HGRN_GROUP):
                    rj = pl.ds(base + ss * HGRN_SUB + j, 1)
                    e = jnp.exp(jnp.minimum(cv[ss] - cum_s[hh, rj, :], 0.0))
                    sc = jnp.sum(qv[ss] * k_s[hh, rj, :] * e, axis=1, keepdims=True)
                    acc[ss] = acc[ss] + jnp.where(sub_row >= j, sc, 0.0) * v_s[hh, rj, :]
            for ss in range(HGRN_GROUP):
                acc_s[hh, blocks[ss], :] = acc[ss]
            return 0

        lax.fori_loop(0, tile // (HGRN_GROUP * HGRN_SUB), group, 0)
        o_ref[:, cols[hh]] = _rms(acc_s[hh], ng_ref[...])


def _hgrn(layer, xb, w, lower_bound, norm_g, batch, seq):
    tile = min(REC_TILE, seq)
    nt = seq // tile
    lev = _hgrn_level_map(tile)
    tri = jnp.asarray(np.tril(np.ones((tile, tile), np.float32)), BF16)
    return pl.pallas_call(
        functools.partial(_hgrn_kernel, tile=tile),
        grid=(batch, nt),
        in_specs=[pl.BlockSpec((tile, D_MODEL), lambda b, i: (b * nt + i, 0)),
                  *[_layer_spec(a, layer) for a in (w, lower_bound, norm_g)],
                  _const_spec(lev.shape), _const_spec(tri.shape)],
        out_specs=pl.BlockSpec((tile, BRANCH_W), lambda b, i: (b * nt + i, 0)),
        out_shape=jax.ShapeDtypeStruct((batch * seq, BRANCH_W), F32),
        scratch_shapes=[pltpu.VMEM((C_HEADS, C_DV, C_DK), F32)]
                       + [pltpu.VMEM((C_HEADS, tile, C_DK), F32) for _ in range(5)],
        compiler_params=_params("arbitrary", "arbitrary"),
        name="hgrn2",
    )(xb, w, lower_bound, norm_g, lev, tri)


def _merge_kernel(x_ref, xb_ref, oa_ref, ob_ref, oc_ref, p_ref, wz_ref, wg_ref, wbr_ref, wout_ref,
                  pgate_ref, pproj_ref, lng_ref, lnb_ref, xo_ref, xbo_ref):
    xb = xb_ref[...]
    z = _dot(xb, wz_ref[...])
    merged = None
    for n, o_ref in enumerate((oa_ref, ob_ref, oc_ref)):
        y = o_ref[...] * _silu(z[:, n * BRANCH_W:(n + 1) * BRANCH_W])
        proj = _dot(y.astype(BF16), wbr_ref[n])
        gate = _sigmoid(_dot(xb, wg_ref[:, n * D_MODEL:(n + 1) * D_MODEL]))
        merged = gate * proj if merged is None else merged + gate * proj
    r = DEEPNORM_ALPHA * x_ref[...] + _dot(merged.astype(BF16), wout_ref[...])
    ple = _dot(p_ref[...].astype(BF16), pproj_ref[...])
    r = r + _sigmoid(_dot(r.astype(BF16), pgate_ref[...])) * ple
    mu = jnp.mean(r, axis=-1, keepdims=True)
    var = jnp.mean(jnp.square(r - mu), axis=-1, keepdims=True)
    xn = (r - mu) * lax.rsqrt(var + NORM_EPS) * lng_ref[...] + lnb_ref[...]
    xo_ref[...] = xn
    xbo_ref[...] = xn.astype(BF16)


def _merge(layer, x, xb, oa, ob, oc, p, wz, wg, wbr, wout, pgate, pproj, lng, lnb):
    t = x.shape[0]
    seq = p.shape[2]
    tm = min(TOKEN_TILE, seq)
    per_seq = seq // tm
    row = lambda i: (i, 0)
    return pl.pallas_call(
        _merge_kernel,
        grid=(t // tm,),
        in_specs=[pl.BlockSpec((tm, D_MODEL), row), pl.BlockSpec((tm, D_MODEL), row),
                  pl.BlockSpec((tm, BRANCH_W), row), pl.BlockSpec((tm, BRANCH_W), row),
                  pl.BlockSpec((tm, BRANCH_W), row),
                  pl.BlockSpec((None, None, tm, PLE_DIM), lambda i: (layer, i // per_seq, i % per_seq, 0)),
                  *[_layer_spec(a, layer) for a in (wz, wg, wbr, wout, pgate, pproj, lng, lnb)]],
        out_specs=[pl.BlockSpec((tm, D_MODEL), row), pl.BlockSpec((tm, D_MODEL), row)],
        out_shape=[jax.ShapeDtypeStruct((t, D_MODEL), F32), jax.ShapeDtypeStruct((t, D_MODEL), BF16)],
        compiler_params=_params("parallel"),
        name="merge",
    )(x, xb, oa, ob, oc, p, wz, wg, wbr, wout, pgate, pproj, lng, lnb)


def _cols(w, idx):
    return w[..., IN_OFFSETS[idx]:IN_OFFSETS[idx + 1]]


def _rot_cols(w):
    half = w.shape[-1] // 2
    return jnp.concatenate([-w[..., half:], w[..., :half]], axis=-1)


def _pad_last(w, left, total):
    return jnp.pad(w, ((0, 0),) * (w.ndim - 1) + ((left, total - left - w.shape[-1]),))


def _mla_weights(w_in, w_uq, w_ukv):
    depth = w_in.shape[0]
    w_kr = _cols(w_in, 2)
    wa = jnp.concatenate([_cols(w_in, 0), _cols(w_in, 1), _pad_last(w_kr, A_NOPE, LANES),
                          _pad_last(_rot_cols(w_kr), A_NOPE, LANES)], axis=-1).astype(BF16)
    uq = w_uq.reshape(depth, A_Q_LORA, A_HEADS, A_NOPE + A_ROPE)
    wq = _pad_last(uq, 0, A_HEAD_PAD).reshape(depth, A_Q_LORA, A_HEADS * A_HEAD_PAD).astype(BF16)
    uq_rot = jnp.concatenate([jnp.zeros_like(uq[..., :A_NOPE]), _rot_cols(uq[..., A_NOPE:])], axis=-1)
    wqr = _pad_last(uq_rot, 0, A_HEAD_PAD).reshape(depth, A_Q_LORA, A_HEADS * A_HEAD_PAD).astype(BF16)
    ukv = w_ukv.reshape(depth, A_KV_LORA, A_HEADS, A_NOPE + A_VDIM)
    wk = _pad_last(ukv[..., :A_NOPE], 0, A_HEAD_PAD).reshape(depth, A_KV_LORA, A_HEADS * A_HEAD_PAD).astype(BF16)
    wv = _pad_last(ukv[..., A_NOPE:], 0, A_HEAD_PAD).reshape(depth, A_KV_LORA, A_HEADS * A_HEAD_PAD).astype(BF16)
    return wa, wq, wqr, wk, wv


def _rope_tables(positions):
    inv = ROPE_THETA ** (-jnp.arange(0, A_ROPE, 2, dtype=F32) / A_ROPE)
    ang = positions.astype(F32).reshape(-1, 1) * inv
    cos, sin = lax.optimization_barrier((jnp.cos(ang), jnp.sin(ang)))
    place = lambda t: _pad_last(jnp.concatenate([t, t], axis=-1), A_NOPE, LANES)
    return place(cos), place(sin)


def _lower_bounds(logits):
    pr = jax.nn.softmax(logits.astype(F32), axis=0)
    return jnp.clip(jnp.cumsum(pr, axis=0) - pr[0:1], 0.0, 1.0 - 1e-6)


def _rows(v):
    return v.astype(F32)[:, None, :]


def kernel(x, p, positions, w_in, a_q_norm, a_w_uq, a_kv_norm, a_w_ukv, b_conv, b_a_log, b_dt_bias, b_norm,
           c_lb_logits, c_norm, w_branch, w_out, ple_proj, ple_gate, ln_g, ln_b):
    batch, seq, _ = x.shape
    t = batch * seq
    depth = w_in.shape[0]
    cos_t, sin_t = _rope_tables(positions)

    wa, wq, wqr, wk, wv = _mla_weights(w_in, a_w_uq, a_w_ukv)
    gate_cols = jnp.concatenate([_cols(w_in, 5), _cols(w_in, 6)], axis=-1)
    wb = jnp.concatenate([_cols(w_in, 4), _pad_last(gate_cols, 0, LANES)], axis=-1).astype(BF16)
    wc = jnp.concatenate([_cols(w_in, 8), _cols(w_in, 9), _cols(w_in, 10)], axis=-1).astype(BF16)
    wz = jnp.concatenate([_cols(w_in, 3), _cols(w_in, 7), _cols(w_in, 11)], axis=-1).astype(BF16)
    wg = _cols(w_in, 12).astype(BF16)
    wbr, wout, pgate, pproj = (w.astype(BF16) for w in (w_branch, w_out, ple_gate, ple_proj))
    qg, kvg, ng_b, ng_c, lng, lnb = (_rows(v) for v in (a_q_norm, a_kv_norm, b_norm, c_norm, ln_g, ln_b))
    conv_w = b_conv.reshape(depth, B_CONV, -1)
    a_log = _rows(_pad_last(b_a_log, 0, LANES))
    dt_bias = _rows(_pad_last(b_dt_bias, 0, LANES))
    lower_bounds = _rows(_lower_bounds(c_lb_logits))

    xf = x.reshape(t, D_MODEL)
    xb = xf.astype(BF16)
    for i in range(depth):
        q, k, v = _mla_prep(i, xb, cos_t, sin_t, wa, qg, kvg, wq, wqr, wk, wv)
        o_a = _mla_flash(q, k, v, batch, seq)
        o_b = _gdn(i, xb, wb, conv_w, a_log, dt_bias, ng_b, batch, seq)
        o_c = _hgrn(i, xb, wc, lower_bounds, ng_c, batch, seq)
        xf, xb = _merge(i, xf, xb, o_a, o_b, o_c, p, wz, wg, wbr, wout, pgate, pproj, lng, lnb)
    return xf.reshape(batch, seq, D_MODEL)
```
